```python
import math
import functools
import jax
import jax.numpy as jnp
from jax import lax
import numpy as np

D_MODEL = 2048
BATCH = 8
SEQ = 2048
DEPTH = 4

CHUNK = 64
N_MIXERS = 3
EPS = 1e-6

HG_F_DIM = 128
HG_HEADS = D_MODEL // HG_F_DIM
HG_I_DIM = D_MODEL // HG_HEADS
HG_FORGET = HG_HEADS * HG_F_DIM
HG_WIDTH = HG_HEADS * HG_I_DIM
HG_IN = 2 * HG_FORGET + 2 * HG_WIDTH

MLA_NOPE = 128
MLA_ROPE = 64
MLA_V = 128
MLA_QK = MLA_NOPE + MLA_ROPE
MLA_HEADS = D_MODEL // MLA_V
MLA_Q_RANK = D_MODEL // 4
MLA_KV_RANK = D_MODEL // 4
MLA_WIDTH = MLA_HEADS * MLA_V
MLA_IN = MLA_Q_RANK + MLA_KV_RANK + MLA_ROPE + MLA_WIDTH
ROPE_THETA = 10000.0
Q_BLOCK = 128

GDN_DK = 128
GDN_DV = 128
GDN_QK_HEADS = D_MODEL // GDN_DK
GDN_V_HEADS = 2 * GDN_QK_HEADS
GDN_KEY_WIDTH = GDN_QK_HEADS * GDN_DK
GDN_WIDTH = GDN_V_HEADS * GDN_DV
GDN_CONV_CH = 2 * GDN_KEY_WIDTH + GDN_WIDTH
GDN_IN = GDN_CONV_CH + GDN_WIDTH + 2 * GDN_V_HEADS
CONV_WIDTH = 4

kernel_name = 'hybrid_hgrn2_mla_gdn_sandwich_trunk'


def rms_norm(x, w):
    xf = x.astype(jnp.float32)
    y = xf * lax.rsqrt(jnp.mean(xf * xf, axis=-1, keepdims=True) + EPS)
    return (y * w.astype(jnp.float32)).astype(x.dtype)


def to_chunks(t):
    b, s, h, d = t.shape
    return t.reshape(b, s // CHUNK, CHUNK, h, d).transpose(1, 0, 3, 2, 4)


def from_chunks(t):
    nc, b, h, c, d = t.shape
    return t.transpose(1, 0, 3, 2, 4).reshape(b, nc * c, h, d)


def hgrn_lower_bounds(lb_param):
    p = jax.nn.softmax(lb_param.astype(jnp.float32), axis=0)
    c = lax.cumsum(p, axis=0)
    return c - c[0:1]


def gla_chunked(q, k, v, log_f):
    f32 = jnp.float32
    qc, kc, vc, gc = (to_chunks(t.astype(f32)) for t in (q, k, v, log_f))
    bcum = lax.cumsum(gc, axis=3)
    tri = jnp.tril(jnp.ones((CHUNK, CHUNK), dtype=bool))

    def step(state, inp):
        qi, ki, vi, bi = inp
        rel = jnp.where(tri[:, :, None], bi[:, :, :, None, :] - bi[:, :, None, :, :], -jnp.inf)
        scores = jnp.einsum('bhtk,bhsk,bhtsk->bhts', qi, ki, jnp.exp(rel))
        b_last = bi[:, :, -1, :]
        out = (jnp.einsum('bhts,bhsv->bhtv', scores, vi)
               + jnp.einsum('bhtk,bhkv->bhtv', qi * jnp.exp(bi), state))
        new_state = (state * jnp.exp(b_last)[..., None]
                     + jnp.einsum('bhsk,bhsv->bhkv', ki * jnp.exp(b_last[:, :, None, :] - bi), vi))
        return new_state, out

    b, h = q.shape[0], q.shape[2]
    state0 = jnp.zeros((b, h, q.shape[-1], v.shape[-1]), f32)
    _, out = lax.scan(step, state0, (qc, kc, vc, bcum))
    return from_chunks(out)


def gated_delta_chunked(q, k, v, g, beta):
    f32 = jnp.float32
    qc, kc, vc = (to_chunks(t.astype(f32)) for t in (q, k, v))
    gc = to_chunks(g.astype(f32)[..., None])[..., 0]
    bc = to_chunks(beta.astype(f32)[..., None])[..., 0]
    gcum = lax.cumsum(gc, axis=3)
    tri = jnp.tril(jnp.ones((CHUNK, CHUNK), dtype=bool))
    strict = jnp.tril(jnp.ones((CHUNK, CHUNK), dtype=bool), k=-1)
    decay = jnp.exp(jnp.where(tri, gcum[..., :, None] - gcum[..., None, :], -jnp.inf))
    k_beta = kc * bc[..., None]
    lower = jnp.where(strict, jnp.einsum('nbhtk,nbhsk->nbhts', k_beta, kc) * decay, 0.0)
    a_mat = lower + jnp.eye(CHUNK, dtype=f32)
    solve = functools.partial(lax.linalg.triangular_solve, left_side=True, lower=True,
                              unit_diagonal=True)
    u = solve(a_mat, vc * bc[..., None])
    w = solve(a_mat, k_beta * jnp.exp(gcum)[..., None])

    def step(state, inp):
        qi, ki, ui, wi, gi, di = inp
        v_new = ui - jnp.einsum('bhtk,bhkv->bhtv', wi, state)
        attn = jnp.einsum('bhtk,bhsk->bhts', qi, ki) * di
        out = (jnp.einsum('bhtk,bhkv->bhtv', qi * jnp.exp(gi)[..., None], state)
               + jnp.einsum('bhts,bhsv->bhtv', attn, v_new))
        g_last = gi[:, :, -1]
        new_state = (state * jnp.exp(g_last)[..., None, None]
                     + jnp.einsum('bhsk,bhsv->bhkv',
                                  ki * jnp.exp(g_last[..., None] - gi)[..., None], v_new))
        return new_state, out

    b, h = q.shape[0], q.shape[2]
    state0 = jnp.zeros((b, h, q.shape[-1], v.shape[-1]), f32)
    _, out = lax.scan(step, state0, (qc, kc, u, w, gcum, decay))
    return from_chunks(out)


def apply_rope(t, cos, sin):
    half = t.shape[-1] // 2
    t1, t2 = t[..., :half], t[..., half:]
    return jnp.concatenate([t1 * cos - t2 * sin, t2 * cos + t1 * sin], axis=-1)


def causal_conv(x, w):
    s = x.shape[1]
    xp = jnp.pad(x, ((0, 0), (CONV_WIDTH - 1, 0), (0, 0)))
    y = xp[:, 0:s] * w[0]
    for j in range(1, CONV_WIDTH):
        y = y + xp[:, j:j + s] * w[j]
    return y


def hgrn2_mixer(u, w_in, head_norm, w_out, lower_bound):
    b, s, _ = u.shape
    proj = u @ w_in
    q, f, i, z = jnp.split(proj, [HG_FORGET, 2 * HG_FORGET, 2 * HG_FORGET + HG_WIDTH], axis=-1)
    f = f.astype(jnp.float32)
    lb = lower_bound.astype(jnp.float32)
    log_forget = jnp.logaddexp(jnp.log(lb), jnp.log1p(-lb) + jax.nn.log_sigmoid(f))
    k = (1.0 - lb) * jax.nn.sigmoid(-f)
    q = jax.nn.silu(q).reshape(b, s, HG_HEADS, HG_F_DIM) * HG_F_DIM ** -0.5
    o = gla_chunked(q, k.reshape(b, s, HG_HEADS, HG_F_DIM),
                    i.reshape(b, s, HG_HEADS, HG_I_DIM),
                    log_forget.reshape(b, s, HG_HEADS, HG_F_DIM))
    o = rms_norm(o, head_norm).astype(u.dtype).reshape(b, s, HG_WIDTH)
    return (o * jax.nn.silu(z)) @ w_out


def mla_mixer(u, positions, w_in, q_norm, kv_norm, w_uq, w_ukv, w_out):
    b, s, _ = u.shape
    proj = u @ w_in
    c_q, c_kv, k_rope, z = jnp.split(
        proj, [MLA_Q_RANK, MLA_Q_RANK + MLA_KV_RANK, MLA_Q_RANK + MLA_KV_RANK + MLA_ROPE], axis=-1)
    q = (rms_norm(c_q, q_norm) @ w_uq).reshape(b, s, MLA_HEADS, MLA_QK)
    kv = (rms_norm(c_kv, kv_norm) @ w_ukv).reshape(b, s, MLA_HEADS, MLA_NOPE + MLA_V)
    q_nope, q_rope = q[..., :MLA_NOPE], q[..., MLA_NOPE:]
    k_nope, v = kv[..., :MLA_NOPE], kv[..., MLA_NOPE:]
    inv_freq = ROPE_THETA ** (-jnp.arange(0, MLA_ROPE, 2, dtype=jnp.float32) / MLA_ROPE)
    ang = positions.astype(jnp.float32)[..., None] * inv_freq
    cos, sin = jnp.cos(ang), jnp.sin(ang)
    q_rope = apply_rope(q_rope, cos[:, :, None, :], sin[:, :, None, :])
    k_rope = apply_rope(k_rope, cos, sin)
    scale = MLA_QK ** -0.5
    n_blocks = s // Q_BLOCK
    key_chunk = jnp.arange(s) // CHUNK

    def blocks(t):
        return t.reshape(b, n_blocks, Q_BLOCK, MLA_HEADS, t.shape[-1]).transpose(1, 0, 2, 3, 4)

    def attend(args):
        blk, qn, qr = args
        scores = (jnp.einsum('bqhd,bkhd->bhqk', qn, k_nope)
                  + jnp.einsum('bqhr,bkr->bhqk', qr, k_rope)) * scale
        q_chunk = (blk * Q_BLOCK + jnp.arange(Q_BLOCK)) // CHUNK
        mask = key_chunk[None, :] <= q_chunk[:, None]
        scores = jnp.where(mask, scores.astype(jnp.float32), -jnp.inf)
        p = jax.nn.softmax(scores, axis=-1).astype(v.dtype)
        return jnp.einsum('bhqk,bkhd->bqhd', p, v)

    out = lax.map(attend, (jnp.arange(n_blocks), blocks(q_nope), blocks(q_rope)))
    o = out.transpose(1, 0, 2, 3, 4).reshape(b, s, MLA_WIDTH)
    return (o * jax.nn.silu(z)) @ w_out


def gdn_mixer(u, w_in, conv_w, a_log, dt_bias, head_norm, w_out):
    b, s, _ = u.shape
    proj = u @ w_in
    qkv, z, a, beta_logit = jnp.split(
        proj, [GDN_CONV_CH, GDN_CONV_CH + GDN_WIDTH, GDN_CONV_CH + GDN_WIDTH + GDN_V_HEADS], axis=-1)
    qkv = jax.nn.silu(causal_conv(qkv, conv_w))
    q, k, v = jnp.split(qkv, [GDN_KEY_WIDTH, 2 * GDN_KEY_WIDTH], axis=-1)
    rep = GDN_V_HEADS // GDN_QK_HEADS

    def l2n(t):
        tf = t.astype(jnp.float32).reshape(b, s, GDN_QK_HEADS, GDN_DK)
        tf = tf * lax.rsqrt(jnp.sum(tf * tf, axis=-1, keepdims=True) + EPS)
        return jnp.repeat(tf, rep, axis=2)

    q = l2n(q) * GDN_DK ** -0.5
    k = l2n(k)
    v = v.reshape(b, s, GDN_V_HEADS, GDN_DV)
    beta = jax.nn.sigmoid(beta_logit.astype(jnp.float32))
    g = -jnp.exp(a_log.astype(jnp.float32)) * jax.nn.softplus(
        a.astype(jnp.float32) + dt_bias.astype(jnp.float32))
    o = gated_delta_chunked(q, k, v, g, beta)
    o = rms_norm(o, head_norm).astype(u.dtype)
    o = o * jax.nn.silu(z.reshape(b, s, GDN_V_HEADS, GDN_DV))
    return o.reshape(b, s, GDN_WIDTH) @ w_out


def setup_inputs(seed: int = 0) -> dict:
    key = jax.random.key(seed)
    keys = iter(jax.random.split(key, 64))

    def nrm(shape, scale):
        return jax.random.normal(next(keys), shape, jnp.float32) * scale

    def gain(n):
        return 1.0 + nrm((n,), 0.02)

    inp = {}
    inp['x'] = nrm((BATCH, SEQ, D_MODEL), 1.0)
    offset = jax.random.randint(next(keys), (BATCH, 1), 0, 4096, dtype=jnp.int32)
    inp['positions'] = offset + jnp.arange(SEQ, dtype=jnp.int32)[None, :]
    inp['hgrn_lb'] = nrm((DEPTH, HG_FORGET), 0.1)
    for i in range(DEPTH):
        kind = i % N_MIXERS
        p = 'l%d_' % i
        inp[p + 'pre_norm'] = gain(D_MODEL)
        inp[p + 'post_norm'] = gain(D_MODEL)
        if kind == 0:
            inp[p + 'w_in'] = nrm((D_MODEL, HG_IN), D_MODEL ** -0.5)
            inp[p + 'head_norm'] = gain(HG_I_DIM)
            inp[p + 'w_out'] = nrm((HG_WIDTH, D_MODEL), HG_WIDTH ** -0.5)
        elif kind == 1:
            inp[p + 'w_in'] = nrm((D_MODEL, MLA_IN), D_MODEL ** -0.5)
            inp[p + 'q_norm'] = gain(MLA_Q_RANK)
            inp[p + 'kv_norm'] = gain(MLA_KV_RANK)
            inp[p + 'w_uq'] = nrm((MLA_Q_RANK, MLA_HEADS * MLA_QK), MLA_Q_RANK ** -0.5)
            inp[p + 'w_ukv'] = nrm((MLA_KV_RANK, MLA_HEADS * (MLA_NOPE + MLA_V)), MLA_KV_RANK ** -0.5)
            inp[p + 'w_out'] = nrm((MLA_WIDTH, D_MODEL), MLA_WIDTH ** -0.5)
        else:
            inp[p + 'w_in'] = nrm((D_MODEL, GDN_IN), D_MODEL ** -0.5)
            inp[p + 'conv_w'] = nrm((CONV_WIDTH, GDN_CONV_CH), CONV_WIDTH ** -0.5)
            inp[p + 'a_log'] = jnp.log(jax.random.uniform(next(keys), (GDN_V_HEADS,), jnp.float32, 1.0, 16.0))
            dt = jnp.exp(jax.random.uniform(next(keys), (GDN_V_HEADS,), jnp.float32,
                                            math.log(1e-3), math.log(1e-1)))
            inp[p + 'dt_bias'] = dt + jnp.log(-jnp.expm1(-dt))
            inp[p + 'head_norm'] = gain(GDN_DV)
            inp[p + 'w_out'] = nrm((GDN_WIDTH, D_MODEL), GDN_WIDTH ** -0.5)
    return inp


def reference(x, positions, hgrn_lb,
              l0_pre_norm, l0_post_norm, l0_w_in, l0_head_norm, l0_w_out,
              l1_pre_norm, l1_post_norm, l1_w_in, l1_q_norm, l1_kv_norm, l1_w_uq, l1_w_ukv, l1_w_out,
              l2_pre_norm, l2_post_norm, l2_w_in, l2_conv_w, l2_a_log, l2_dt_bias, l2_head_norm, l2_w_out,
              l3_pre_norm, l3_post_norm, l3_w_in, l3_head_norm, l3_w_out):
    lower_bounds = hgrn_lower_bounds(hgrn_lb)
    layers = [
        (l0_pre_norm, l0_post_norm, (l0_w_in, l0_head_norm, l0_w_out)),
        (l1_pre_norm, l1_post_norm, (l1_w_in, l1_q_norm, l1_kv_norm, l1_w_uq, l1_w_ukv, l1_w_out)),
        (l2_pre_norm, l2_post_norm, (l2_w_in, l2_conv_w, l2_a_log, l2_dt_bias, l2_head_norm, l2_w_out)),
        (l3_pre_norm, l3_post_norm, (l3_w_in, l3_head_norm, l3_w_out)),
    ]
    h = x
    for i in range(DEPTH):
        pre, post, params = layers[i]
        u = rms_norm(h, pre)
        kind = i % N_MIXERS
        if kind == 0:
            y = hgrn2_mixer(u, *params, lower_bound=lower_bounds[i])
        elif kind == 1:
            y = mla_mixer(u, positions, *params)
        else:
            y = gdn_mixer(u, *params)
        h = h + rms_norm(y, post)
    return h
```

```python
import functools
import math

import jax
import jax.numpy as jnp
from jax import lax
from jax.experimental import pallas as pl
from jax.experimental.pallas import tpu as pltpu

F32 = jnp.float32
BF16 = jnp.bfloat16

EPS = 1e-6
CHUNK = 64
SUB = 16
N_SUB = CHUNK // SUB
HEAD_DIM = 128
DEPTH = 4
ROPE_HALF = 32
ROPE_THETA = 10000.0
MLA_SCALE = (128 + 64) ** -0.5
CONV_WIDTH = 4
MAX_EXP_ARG = 80.0
VMEM_LIMIT_BYTES = 56 * 1024 * 1024

NT_DIMS = (((1,), (1,)), ((), ()))
TN_DIMS = (((0,), (0,)), ((), ()))


def _params(*semantics):
    return pltpu.CompilerParams(dimension_semantics=semantics,
                                vmem_limit_bytes=VMEM_LIMIT_BYTES)


def _sigmoid(x):
    return 1.0 / (1.0 + jnp.exp(-x))


def _silu(x):
    return x * _sigmoid(x)


def _rms(x, gain):
    ms = jnp.mean(x * x, axis=-1, keepdims=True)
    return x * lax.rsqrt(ms + EPS) * gain


def _dot(a, b):
    return jnp.dot(a, b, preferred_element_type=F32)


def _dot_nt(a, b):
    return lax.dot_general(a, b, NT_DIMS, preferred_element_type=F32)


def _dot_tn(a, b):
    return lax.dot_general(a, b, TN_DIMS, preferred_element_type=F32)


def _chunk_cumsum(x, row):
    shift = 1
    while shift < CHUNK:
        x = x + jnp.where(row >= shift, pltpu.roll(x, shift, axis=0), 0.0)
        shift *= 2
    return x


def _norm_matmul_body(x_ref, g_ref, w_ref, o_ref, xn_ref):
    @pl.when(pl.program_id(1) == 0)
    def _():
        xn_ref[...] = _rms(x_ref[...], g_ref[...]).astype(BF16)

    o_ref[...] = _dot(xn_ref[...], w_ref[...]).astype(o_ref.dtype)


def norm_matmul(x, col_block, gain, w, *, tm, tn, out_dtype, name):
    t = x.shape[0]
    k, n = w.shape
    tm, tn = min(tm, t), min(tn, n)
    return pl.pallas_call(
        _norm_matmul_body,
        grid=(t // tm, n // tn),
        in_specs=[pl.BlockSpec((tm, k), lambda i, j: (i, col_block)),
                  pl.BlockSpec((1, k), lambda i, j: (0, 0)),
                  pl.BlockSpec((k, tn), lambda i, j: (0, j))],
        out_specs=pl.BlockSpec((tm, tn), lambda i, j: (i, j)),
        out_shape=jax.ShapeDtypeStruct((t, n), out_dtype),
        scratch_shapes=[pltpu.VMEM((tm, k), BF16)],
        compiler_params=_params("parallel", "arbitrary"),
        name=name,
    )(x, gain.reshape(1, k), w)


def _out_proj_body(g_ref, w_ref, p_ref, h_ref, o_ref):
    y = _dot(g_ref[...], w_ref[...])
    o_ref[...] = h_ref[...] + _rms(y, p_ref[...])


def out_proj_residual(g, w, post, h, *, tm, name):
    t, k = g.shape
    d = w.shape[1]
    tm = min(tm, t)
    return pl.pallas_call(
        _out_proj_body,
        grid=(t // tm,),
        in_specs=[pl.BlockSpec((tm, k), lambda i: (i, 0)),
                  pl.BlockSpec((k, d), lambda i: (0, 0), pipeline_mode=pl.Buffered(1)),
                  pl.BlockSpec((1, d), lambda i: (0, 0)),
                  pl.BlockSpec((tm, d), lambda i: (i, 0))],
        out_specs=pl.BlockSpec((tm, d), lambda i: (i, 0)),
        out_shape=jax.ShapeDtypeStruct((t, d), F32),
        compiler_params=_params("parallel"),
        name=name,
    )(g, w, post.reshape(1, d), h)


def _hgrn_body(lbp_ref, hn_ref, q_ref, f_ref, i_ref, z_ref, o_ref, *, layer, n_chunks):
    p = lbp_ref[...]
    e = jnp.exp(p - jnp.max(p, axis=0, keepdims=True))
    sm = e / jnp.sum(e, axis=0, keepdims=True)
    lb = jnp.zeros((1, HEAD_DIM), F32)
    for j in range(1, layer + 1):
        lb = lb + sm[j:j + 1]
    log_lb = jnp.log(lb)
    log_1m_lb = jnp.log(1.0 - lb)
    one_m_lb = 1.0 - lb
    hn = hn_ref[...]

    row = lax.broadcasted_iota(jnp.int32, (CHUNK, HEAD_DIM), 0)
    t_idx = lax.broadcasted_iota(jnp.int32, (CHUNK, N_SUB * CHUNK), 0)
    c_idx = lax.broadcasted_iota(jnp.int32, (CHUNK, N_SUB * CHUNK), 1)
    valid = ((c_idx // CHUNK) == (t_idx // SUB)) & ((c_idx % CHUNK) <= t_idx)
    scale = HEAD_DIM ** -0.5

    def chunk(c, state_t):
        rows = pl.ds(pl.multiple_of(c * CHUNK, CHUNK), CHUNK)
        q = q_ref[rows, :]
        f = f_ref[rows, :]
        v = i_ref[rows, :]
        z = z_ref[rows, :]

        ef = jnp.exp(-jnp.abs(f))
        inv = 1.0 / (1.0 + ef)
        log_sig = jnp.minimum(f, 0.0) - jnp.log(1.0 + ef)
        cc = log_1m_lb + log_sig
        log_f = jnp.maximum(log_lb, cc) + jnp.log(1.0 + jnp.exp(-jnp.abs(log_lb - cc)))
        k = one_m_lb * jnp.where(f >= 0.0, ef * inv, inv)
        qs = _silu(q) * scale

        b = _chunk_cumsum(log_f, row)
        refs = [jnp.zeros((1, HEAD_DIM), F32)] + [b[SUB * i - 1:SUB * i] for i in range(1, N_SUB)]
        ref_rows = jnp.concatenate([jnp.broadcast_to(r, (SUB, HEAD_DIM)) for r in refs], axis=0)
        q_in = (qs * jnp.exp(b - ref_rows)).astype(BF16)
        k_stack = jnp.concatenate(
            [k * jnp.exp(jnp.minimum(r - b, MAX_EXP_ARG)) for r in refs], axis=0).astype(BF16)
        scores = _dot_nt(q_in, k_stack)
        scores = jnp.where(valid, scores, 0.0).astype(BF16)
        v16 = v.astype(BF16)
        o = _dot(scores, jnp.concatenate([v16] * N_SUB, axis=0))
        o = o + _dot_nt((qs * jnp.exp(b)).astype(BF16), state_t.astype(BF16))

        b_last = b[CHUNK - 1:CHUNK]
        k_out = (k * jnp.exp(b_last - b)).astype(BF16)
        state_t = state_t * jnp.exp(b_last) + _dot_tn(v16, k_out)

        o_ref[rows, :] = (_rms(o, hn) * _silu(z)).astype(o_ref.dtype)
        return state_t

    lax.fori_loop(0, n_chunks, chunk, jnp.zeros((HEAD_DIM, HEAD_DIM), F32))


def hgrn_mixer(proj, lb_param, head_norm, *, batch, seq, layer):
    heads = lb_param.shape[1] // HEAD_DIM
    col = lambda off: pl.BlockSpec((seq, HEAD_DIM), lambda b, h: (b, off * heads + h))
    return pl.pallas_call(
        functools.partial(_hgrn_body, layer=layer, n_chunks=seq // CHUNK),
        grid=(batch, heads),
        in_specs=[pl.BlockSpec((DEPTH, HEAD_DIM), lambda b, h: (0, h)),
                  pl.BlockSpec((1, HEAD_DIM), lambda b, h: (0, 0)),
                  col(0), col(1), col(2), col(3)],
        out_specs=pl.BlockSpec((seq, HEAD_DIM), lambda b, h: (b, h)),
        out_shape=jax.ShapeDtypeStruct((batch * seq, heads * HEAD_DIM), BF16),
        compiler_params=_params("parallel", "parallel"),
        name="hgrn_gla_l%d" % layer,
    )(lb_param, head_norm.reshape(1, HEAD_DIM), proj, proj, proj, proj)


def _rope_tables(pos_col):
    lane = lax.broadcasted_iota(jnp.int32, (1, HEAD_DIM), 1)
    j = (lane % (2 * ROPE_HALF)).astype(F32)
    inv_freq = jnp.exp(j * (-2.0 * math.log(ROPE_THETA) / (2 * ROPE_HALF)))
    live = (lane % (2 * ROPE_HALF)) < ROPE_HALF
    ang = pos_col * inv_freq
    cos = jnp.where(live, jnp.cos(ang), 0.0)
    sin = jnp.where(live, jnp.sin(ang), 0.0)
    sin = jnp.where(lane < 2 * ROPE_HALF, -sin, sin)
    return cos, sin


def _apply_rope(t, cos, sin):
    return t * cos + pltpu.roll(t, 2 * ROPE_HALF, axis=1) * sin


def _rope_prep_body(pos_ref, kr_ref, cos_ref, sin_ref, k_ref):
    cos, sin = _rope_tables(pos_ref[...].astype(F32))
    cos_ref[...] = cos
    sin_ref[...] = sin
    k_ref[...] = _apply_rope(kr_ref[...], cos, sin).astype(BF16)


def rope_prep(pos_col, proj, kr_block, *, tm):
    t = pos_col.shape[0]
    tm = min(tm, t)
    tab = pl.BlockSpec((tm, HEAD_DIM), lambda i: (i, 0))
    return pl.pallas_call(
        _rope_prep_body,
        grid=(t // tm,),
        in_specs=[pl.BlockSpec((tm, 1), lambda i: (i, 0)),
                  pl.BlockSpec((tm, HEAD_DIM), lambda i: (i, kr_block))],
        out_specs=[tab, tab, tab],
        out_shape=[jax.ShapeDtypeStruct((t, HEAD_DIM), F32),
                   jax.ShapeDtypeStruct((t, HEAD_DIM), F32),
                   jax.ShapeDtypeStruct((t, HEAD_DIM), BF16)],
        compiler_params=_params("parallel"),
        name="mla_rope_prep",
    )(pos_col, proj)


def _mla_attn_body(q_ref, cos_ref, sin_ref, kv_ref, kr_ref, z_ref, o_ref, *, tq):
    qi = pl.program_id(2)
    q = q_ref[...]
    q_rope = _apply_rope(q[:, HEAD_DIM:], cos_ref[...], sin_ref[...])
    q_cat = (jnp.concatenate([q[:, :HEAD_DIM], q_rope], axis=1) * MLA_SCALE).astype(BF16)

    def scores_for(j):
        rows = pl.ds(pl.multiple_of(j * tq, tq), tq)
        k_cat = jnp.concatenate([kv_ref[rows, :HEAD_DIM], kr_ref[rows, :]], axis=1)
        return _dot_nt(q_cat, k_cat), kv_ref[rows, HEAD_DIM:]

    def update(carry, s, v):
        m, l, acc = carry
        m_new = jnp.maximum(m, jnp.max(s, axis=-1, keepdims=True))
        alpha = jnp.exp(m - m_new)
        p = jnp.exp(s - m_new)
        l = alpha * l + jnp.sum(p, axis=-1, keepdims=True)
        acc = alpha * acc + _dot(p.astype(BF16), v)
        return m_new, l, acc

    def full_tile(j, carry):
        s, v = scores_for(j)
        return update(carry, s, v)

    init = (jnp.full((tq, 1), -jnp.inf, F32), jnp.zeros((tq, 1), F32),
            jnp.zeros((tq, HEAD_DIM), F32))
    carry = lax.fori_loop(0, qi, full_tile, init)
    s, v = scores_for(qi)
    q_chunk = lax.broadcasted_iota(jnp.int32, (tq, tq), 0) // CHUNK
    k_chunk = lax.broadcasted_iota(jnp.int32, (tq, tq), 1) // CHUNK
    s = jnp.where(k_chunk <= q_chunk, s, -jnp.inf)
    m, l, acc = update(carry, s, v)
    o_ref[...] = (acc / l * _silu(z_ref[...])).astype(o_ref.dtype)


def mla_attention(q_up, cos, sin, kv_up, k_rope, proj, z_block0, *, batch, seq, heads, tq):
    tq = min(tq, seq)
    nq = seq // tq
    return pl.pallas_call(
        functools.partial(_mla_attn_body, tq=tq),
        grid=(batch, heads, nq),
        in_specs=[pl.BlockSpec((tq, 2 * HEAD_DIM), lambda b, h, i: (b * nq + i, h)),
                  pl.BlockSpec((tq, HEAD_DIM), lambda b, h, i: (b * nq + i, 0)),
                  pl.BlockSpec((tq, HEAD_DIM), lambda b, h, i: (b * nq + i, 0)),
                  pl.BlockSpec((seq, 2 * HEAD_DIM), lambda b, h, i: (b, h)),
                  pl.BlockSpec((seq, HEAD_DIM), lambda b, h, i: (b, 0)),
                  pl.BlockSpec((tq, HEAD_DIM), lambda b, h, i: (b * nq + i, z_block0 + h))],
        out_specs=pl.BlockSpec((tq, HEAD_DIM), lambda b, h, i: (b * nq + i, h)),
        out_shape=jax.ShapeDtypeStruct((batch * seq, heads * HEAD_DIM), BF16),
        compiler_params=_params("parallel", "parallel", "arbitrary"),
        name="mla_attention",
    )(q_up, cos, sin, kv_up, k_rope, proj)


def _gdn_gates_body(ab_ref, alog_ref, dt_ref, o_ref, *, n_heads):
    ab = ab_ref[...]
    x = ab[:, :n_heads] + dt_ref[...]
    softplus = jnp.maximum(x, 0.0) + jnp.log(1.0 + jnp.exp(-jnp.abs(x)))
    g = -jnp.exp(alog_ref[...]) * softplus
    row = lax.broadcasted_iota(jnp.int32, g.shape, 0) % CHUNK
    o_ref[...] = jnp.concatenate([_chunk_cumsum(g, row), _sigmoid(ab[:, n_heads:])], axis=1)


def gdn_gates(ab, a_log, dt_bias, *, batch, seq):
    n_heads = a_log.shape[0]
    vec = pl.BlockSpec((1, n_heads), lambda b: (0, 0))
    return pl.pallas_call(
        functools.partial(_gdn_gates_body, n_heads=n_heads),
        grid=(batch,),
        in_specs=[pl.BlockSpec((seq, 2 * n_heads), lambda b: (b, 0)), vec, vec],
        out_specs=pl.BlockSpec((seq, 2 * n_heads), lambda b: (b, 0)),
        out_shape=jax.ShapeDtypeStruct((batch * seq, 2 * n_heads), F32),
        compiler_params=_params("parallel"),
        name="gdn_gates",
    )(ab, a_log.reshape(1, n_heads), dt_bias.reshape(1, n_heads))


def _tri_inverse(low):
    eye = (lax.broadcasted_iota(jnp.int32, (CHUNK, CHUNK), 0) ==
           lax.broadcasted_iota(jnp.int32, (CHUNK, CHUNK), 1)).astype(F32)
    inv = eye - low
    power = low
    n = 2
    while n < CHUNK:
        p16 = power.astype(BF16)
        power = _dot(p16, p16)
        inv = _dot(inv.astype(BF16), (eye + power).astype(BF16))
        n *= 2
    return inv


def _gdn_body(cw_ref, hn_ref, q_ref, k_ref, v_ref, z_ref, gcol_ref, grow_ref, o_ref,
              u_s, w_s, qe_s, kl_s, at_s, *, n_chunks):
    cw = cw_ref[...]
    hn = hn_ref[...]
    row = lax.broadcasted_iota(jnp.int32, (CHUNK, CHUNK), 0)
    lane = lax.broadcasted_iota(jnp.int32, (CHUNK, CHUNK), 1)
    tri = lane <= row
    strict = lane < row

    def conv_silu(x_ref, c, w):
        start = pl.multiple_of(c * CHUNK, CHUNK)
        cur = x_ref[pl.ds(start, CHUNK), :]
        prev_start = pl.multiple_of(jnp.maximum(start - 8, 0), 8)
        prev = x_ref[pl.ds(prev_start, 8), :] * (c > 0).astype(F32)
        ext = jnp.concatenate([prev, cur], axis=0)
        y = cur * w[CONV_WIDTH - 1:CONV_WIDTH]
        for d in range(1, CONV_WIDTH):
            y = y + ext[8 - d:8 - d + CHUNK] * w[CONV_WIDTH - 1 - d:CONV_WIDTH - d]
        return _silu(y)

    def l2n(x):
        return x * lax.rsqrt(jnp.sum(x * x, axis=-1, keepdims=True) + EPS)

    def prepare(c, _):
        rows = pl.ds(pl.multiple_of(c * CHUNK, CHUNK), CHUNK)
        q = l2n(conv_silu(q_ref, c, cw[:, :HEAD_DIM])) * (HEAD_DIM ** -0.5)
        k = l2n(conv_silu(k_ref, c, cw[:, HEAD_DIM:2 * HEAD_DIM]))
        v = conv_silu(v_ref, c, cw[:, 2 * HEAD_DIM:])
        k16 = k.astype(BF16)
        qk_kk = _dot_nt(jnp.concatenate([q.astype(BF16), k16], axis=0), k16)
        qk = qk_kk[:CHUNK]
        kk = qk_kk[CHUNK:]
        gcol = gcol_ref[0, 0, rows, :]
        for hh in range(2):
            gc = gcol[:, hh:hh + 1]
            beta = gcol[:, 2 + hh:3 + hh]
            gr = grow_ref[0, hh, pl.ds(c, 1), :]
            decay = jnp.exp(jnp.minimum(gc - gr, 0.0))
            at_s[hh, rows, :] = jnp.where(tri, qk * decay, 0.0).astype(BF16)
            low = jnp.where(strict, kk * decay, 0.0) * beta
            inv = _tri_inverse(low)
            e_gc = jnp.exp(gc)
            rhs = jnp.concatenate([v[:, hh * HEAD_DIM:(hh + 1) * HEAD_DIM] * beta,
                                   k * (beta * e_gc)], axis=1)
            uw = _dot(inv.astype(BF16), rhs.astype(BF16))
            u_s[hh, rows, :] = uw[:, :HEAD_DIM]
            w_s[hh, rows, :] = uw[:, HEAD_DIM:].astype(BF16)
            qe_s[hh, rows, :] = (q * e_gc).astype(BF16)
            kl_s[hh, rows, :] = (k * jnp.exp(gc[CHUNK - 1:CHUNK] - gc)).astype(BF16)
        return 0

    lax.fori_loop(0, n_chunks, prepare, 0)

    def recur(c, states):
        rows = pl.ds(pl.multiple_of(c * CHUNK, CHUNK), CHUNK)
        z = z_ref[rows, :]
        outs = []
        new_states = []
        for hh in range(2):
            state = states[hh]
            s16 = state.astype(BF16)
            v_new = u_s[hh, rows, :] - _dot(w_s[hh, rows, :], s16)
            v16 = v_new.astype(BF16)
            o = _dot(qe_s[hh, rows, :], s16) + _dot(at_s[hh, rows, :], v16)
            g_last = gcol_ref[0, 0, pl.ds(c * CHUNK + CHUNK - 1, 1), hh:hh + 1]
            new_states.append(state * jnp.exp(g_last) + _dot_tn(kl_s[hh, rows, :], v16))
            outs.append(_rms(o, hn) * _silu(z[:, hh * HEAD_DIM:(hh + 1) * HEAD_DIM]))
        o_ref[rows, :] = jnp.concatenate(outs, axis=1).astype(o_ref.dtype)
        return tuple(new_states)

    zero = jnp.zeros((HEAD_DIM, HEAD_DIM), F32)
    lax.fori_loop(0, n_chunks, recur, (zero, zero))


def gdn_mixer(proj, conv_w, head_norm, gcol, grow, *, batch, seq, qk_heads):
    nc = seq // CHUNK
    key_w = qk_heads * HEAD_DIM
    cw = jnp.concatenate([
        conv_w[:, :key_w].reshape(CONV_WIDTH, qk_heads, HEAD_DIM),
        conv_w[:, key_w:2 * key_w].reshape(CONV_WIDTH, qk_heads, HEAD_DIM),
        conv_w[:, 2 * key_w:].reshape(CONV_WIDTH, qk_heads, 2 * HEAD_DIM)], axis=2)
    cw = cw.transpose(1, 0, 2)
    v_block0 = 2 * qk_heads // 2
    z_block0 = v_block0 + qk_heads
    scratch16 = pltpu.VMEM((2, seq, HEAD_DIM), BF16)
    return pl.pallas_call(
        functools.partial(_gdn_body, n_chunks=nc),
        grid=(batch, qk_heads),
        in_specs=[pl.BlockSpec((None, CONV_WIDTH, 4 * HEAD_DIM), lambda b, g: (g, 0, 0)),
                  pl.BlockSpec((1, HEAD_DIM), lambda b, g: (0, 0)),
                  pl.BlockSpec((seq, HEAD_DIM), lambda b, g: (b, g)),
                  pl.BlockSpec((seq, HEAD_DIM), lambda b, g: (b, qk_heads + g)),
                  pl.BlockSpec((seq, 2 * HEAD_DIM), lambda b, g: (b, v_block0 + g)),
                  pl.BlockSpec((seq, 2 * HEAD_DIM), lambda b, g: (b, z_block0 + g)),
                  pl.BlockSpec((1, 1, seq, 4), lambda b, g: (b, g, 0, 0)),
                  pl.BlockSpec((1, 2, nc, CHUNK), lambda b, g: (b, g, 0, 0))],
        out_specs=pl.BlockSpec((seq, 2 * HEAD_DIM), lambda b, g: (b, g)),
        out_shape=jax.ShapeDtypeStruct((batch * seq, 2 * qk_heads * HEAD_DIM), BF16),
        scratch_shapes=[pltpu.VMEM((2, seq, HEAD_DIM), F32), scratch16, scratch16, scratch16,
                        pltpu.VMEM((2, seq, CHUNK), BF16)],
        compiler_params=_params("parallel", "parallel"),
        name="gdn_delta_rule",
    )(cw, head_norm.reshape(1, HEAD_DIM), proj, proj, proj, proj, gcol, grow)


def _hgrn_layer(h, lb_param, pre, post, w_in, head_norm, w_out, *, batch, seq, layer):
    proj = norm_matmul(h, 0, pre, w_in.astype(BF16), tm=1024, tn=1024, out_dtype=F32,
                       name="hgrn_in_proj_l%d" % layer)
    gated = hgrn_mixer(proj, lb_param, head_norm, batch=batch, seq=seq, layer=layer)
    return out_proj_residual(gated, w_out.astype(BF16), post, h, tm=512,
                             name="hgrn_out_proj_l%d" % layer)


def _pad_rope_cols(w):
    zeros = jnp.zeros(w.shape[:-1] + (ROPE_HALF,), w.dtype)
    return jnp.concatenate([w[..., :ROPE_HALF], zeros, w[..., ROPE_HALF:], zeros], axis=-1)


def _mla_layer(h, positions, pre, post, w_in, q_norm, kv_norm, w_uq, w_ukv, w_out, *, batch, seq):
    d_model = h.shape[1]
    q_rank = q_norm.shape[0]
    kv_rank = kv_norm.shape[0]
    heads = w_out.shape[0] // HEAD_DIM
    lat = q_rank + kv_rank
    w_in_p = jnp.concatenate([w_in[:, :lat], _pad_rope_cols(w_in[:, lat:lat + 2 * ROPE_HALF]),
                              w_in[:, lat + 2 * ROPE_HALF:]], axis=1).astype(BF16)
    proj = norm_matmul(h, 0, pre, w_in_p, tm=1024, tn=640, out_dtype=F32, name="mla_in_proj")
    w_uq_h = w_uq.reshape(q_rank, heads, HEAD_DIM + 2 * ROPE_HALF)
    w_uq_p = jnp.concatenate([w_uq_h[..., :HEAD_DIM], _pad_rope_cols(w_uq_h[..., HEAD_DIM:])],
                             axis=-1).reshape(q_rank, heads * 2 * HEAD_DIM).astype(BF16)
    q_up = norm_matmul(proj, 0, q_norm, w_uq_p, tm=1024, tn=1024, out_dtype=F32, name="mla_q_up")
    kv_up = norm_matmul(proj, kv_rank // q_rank, kv_norm, w_ukv.astype(BF16), tm=1024, tn=1024,
                        out_dtype=BF16, name="mla_kv_up")
    pos_col = positions.reshape(batch * seq, 1)
    cos, sin, k_rope = rope_prep(pos_col, proj, lat // HEAD_DIM, tm=1024)
    gated = mla_attention(q_up, cos, sin, kv_up, k_rope, proj, lat // HEAD_DIM + 1,
                          batch=batch, seq=seq, heads=heads, tq=256)
    del d_model
    return out_proj_residual(gated, w_out.astype(BF16), post, h, tm=512, name="mla_out_proj")


def _gdn_layer(h, pre, post, w_in, conv_w, a_log, dt_bias, head_norm, w_out, *, batch, seq):
    v_heads = a_log.shape[0]
    qk_heads = v_heads // 2
    main = 2 * qk_heads * HEAD_DIM + 2 * v_heads * HEAD_DIM
    proj = norm_matmul(h, 0, pre, w_in[:, :main].astype(BF16), tm=1024, tn=1024, out_dtype=F32,
                       name="gdn_in_proj")
    w_ab = jnp.pad(w_in[:, main:], ((0, 0), (0, HEAD_DIM - 2 * v_heads))).astype(BF16)
    ab = norm_matmul(h, 0, pre, w_ab, tm=1024, tn=HEAD_DIM, out_dtype=F32,
                     name="gdn_gate_proj")[:, :2 * v_heads]
    nc = seq // CHUNK
    col = gdn_gates(ab, a_log, dt_bias, batch=batch, seq=seq)
    gcol = col.reshape(batch, seq, 2, qk_heads, 2).transpose(0, 3, 1, 2, 4)
    gcol = gcol.reshape(batch, qk_heads, seq, 4)
    grow = col[:, :v_heads].reshape(batch, nc, CHUNK, v_heads).transpose(0, 3, 1, 2)
    gated = gdn_mixer(proj, conv_w, head_norm, gcol, grow, batch=batch, seq=seq, qk_heads=qk_heads)
    return out_proj_residual(gated, w_out.astype(BF16), post, h, tm=512, name="gdn_out_proj")


def kernel(x, positions, hgrn_lb, l0_pre_norm, l0_post_norm, l0_w_in, l0_head_norm, l0_w_out, l1_pre_norm, l1_post_norm, l1_w_in, l1_q_norm, l1_kv_norm, l1_w_uq, l1_w_ukv, l1_w_out, l2_pre_norm, l2_post_norm, l2_w_in, l2_conv_w, l2_a_log, l2_dt_bias, l2_head_norm, l2_w_out, l3_pre_norm, l3_post_norm, l3_w_in, l3_head_norm, l3_w_out):
    batch, seq, d_model = x.shape
    h = x.reshape(batch * seq, d_model)
    h = _hgrn_layer(h, hgrn_lb, l0_pre_norm, l0_post_norm, l0_w_in, l0_head_norm, l0_w_out,
                    batch=batch, seq=seq, layer=0)
    h = _mla_layer(h, positions, l1_pre_norm, l1_post_norm, l1_w_in, l1_q_norm, l1_kv_norm,
                   l1_w_uq, l1_w_ukv, l1_w_out, batch=batch, seq=seq)
    h = _gdn_layer(h, l2_pre_norm, l2_post_norm, l2_w_in, l2_conv_w, l2_a_log, l2_dt_bias,
                   l2_head_norm, l2_w_out, batch=batch, seq=seq)
    h = _hgrn_layer(h, hgrn_lb, l3_pre_norm, l3_post_norm, l3_w_in, l3_head_norm, l3_w_out,
                    batch=batch, seq=seq, layer=3)
    return h.reshape(batch, seq, d_model)
```

```python
import functools
import math

import jax
import jax.numpy as jnp
from jax import lax
from jax.experimental import pallas as pl
from jax.experimental.pallas import tpu as pltpu

F32 = jnp.float32
BF16 = jnp.bfloat16

EPS = 1e-6
CHUNK = 64
SUB = 16
N_SUB = CHUNK // SUB
HEAD_DIM = 128
DEPTH = 4
ROPE_HALF = 32
ROPE_THETA = 10000.0
MLA_SCALE = (128 + 64) ** -0.5
CONV_WIDTH = 4
MAX_EXP_ARG = 80.0
VMEM_LIMIT_BYTES = 56 * 1024 * 1024
HGRN_UNROLL = 4
GDN_PREP_UNROLL = 4
GDN_INVERT_UNROLL = 8
GDN_RECUR_UNROLL = 2
MLA_TQ = 256
MLA_TK = 512
MLA_HEADS_PER_STEP = 2

NT_DIMS = (((1,), (1,)), ((), ()))
TN_DIMS = (((0,), (0,)), ((), ()))


def _params(*semantics):
    return pltpu.CompilerParams(dimension_semantics=semantics,
                                vmem_limit_bytes=VMEM_LIMIT_BYTES)


def _sigmoid(x):
    return 1.0 / (1.0 + jnp.exp(-x))


def _silu(x):
    return x * _sigmoid(x)


def _rms(x, gain):
    ms = jnp.mean(x * x, axis=-1, keepdims=True)
    return x * lax.rsqrt(ms + EPS) * gain


def _dot(a, b):
    return jnp.dot(a, b, preferred_element_type=F32)


def _dot_nt(a, b):
    return lax.dot_general(a, b, NT_DIMS, preferred_element_type=F32)


def _dot_tn(a, b):
    return lax.dot_general(a, b, TN_DIMS, preferred_element_type=F32)


def _stack(xs):
    m, n = xs[0].shape
    return jnp.concatenate(xs, axis=0).reshape(len(xs), m, n)


def _bdot(a, b):
    return lax.dot_general(a, b, (((2,), (1,)), ((0,), (0,))), preferred_element_type=F32)


def _bdot_nt(a, b):
    return lax.dot_general(a, b, (((2,), (2,)), ((0,), (0,))), preferred_element_type=F32)


def _bdot_tn(a, b):
    return lax.dot_general(a, b, (((1,), (1,)), ((0,), (0,))), preferred_element_type=F32)


def _chunk_cumsum(x, row):
    shift = 1
    while shift < CHUNK:
        x = x + jnp.where(row >= shift, pltpu.roll(x, shift, axis=0), 0.0)
        shift *= 2
    return x


def _norm_matmul_body(x_ref, g_ref, w_ref, o_ref, xn_ref):
    @pl.when(pl.program_id(1) == 0)
    def _():
        xn_ref[...] = _rms(x_ref[...], g_ref[...]).astype(BF16)

    o_ref[...] = _dot(xn_ref[...], w_ref[...]).astype(o_ref.dtype)


def norm_matmul(x, col_block, gain, w, *, tm, tn, out_dtype, name):
    t = x.shape[0]
    k, n = w.shape
    tm, tn = min(tm, t), min(tn, n)
    return pl.pallas_call(
        _norm_matmul_body,
        grid=(t // tm, n // tn),
        in_specs=[pl.BlockSpec((tm, k), lambda i, j: (i, col_block)),
                  pl.BlockSpec((1, k), lambda i, j: (0, 0)),
                  pl.BlockSpec((k, tn), lambda i, j: (0, j))],
        out_specs=pl.BlockSpec((tm, tn), lambda i, j: (i, j)),
        out_shape=jax.ShapeDtypeStruct((t, n), out_dtype),
        scratch_shapes=[pltpu.VMEM((tm, k), BF16)],
        compiler_params=_params("parallel", "arbitrary"),
        name=name,
    )(x, gain.reshape(1, k), w)


def _out_proj_body(g_ref, w_ref, p_ref, h_ref, o_ref):
    y = _dot(g_ref[...], w_ref[...])
    o_ref[...] = h_ref[...] + _rms(y, p_ref[...])


def out_proj_residual(g, w, post, h, *, tm, name):
    t, k = g.shape
    d = w.shape[1]
    tm = min(tm, t)
    return pl.pallas_call(
        _out_proj_body,
        grid=(t // tm,),
        in_specs=[pl.BlockSpec((tm, k), lambda i: (i, 0)),
                  pl.BlockSpec((k, d), lambda i: (0, 0), pipeline_mode=pl.Buffered(1)),
                  pl.BlockSpec((1, d), lambda i: (0, 0)),
                  pl.BlockSpec((tm, d), lambda i: (i, 0))],
        out_specs=pl.BlockSpec((tm, d), lambda i: (i, 0)),
        out_shape=jax.ShapeDtypeStruct((t, d), F32),
        compiler_params=_params("parallel"),
        name=name,
    )(g, w, post.reshape(1, d), h)


def _hgrn_body(lbp_ref, hn_ref, q_ref, f_ref, i_ref, z_ref, o_ref, *, layer, n_chunks):
    p = lbp_ref[...]
    e = jnp.exp(p - jnp.max(p, axis=0, keepdims=True))
    sm = e / jnp.sum(e, axis=0, keepdims=True)
    lb = jnp.zeros((1, HEAD_DIM), F32)
    for j in range(1, layer + 1):
        lb = lb + sm[j:j + 1]
    log_lb = jnp.log(lb)
    log_1m_lb = jnp.log(1.0 - lb)
    one_m_lb = 1.0 - lb
    hn = hn_ref[...]

    n_par = math.gcd(n_chunks, HGRN_UNROLL)
    span = n_par * CHUNK
    row = lax.broadcasted_iota(jnp.int32, (span, HEAD_DIM), 0) % CHUNK
    t_idx = lax.broadcasted_iota(jnp.int32, (CHUNK, N_SUB * CHUNK), 0)
    c_idx = lax.broadcasted_iota(jnp.int32, (CHUNK, N_SUB * CHUNK), 1)
    valid = ((c_idx // CHUNK) == (t_idx // SUB)) & ((c_idx % CHUNK) <= t_idx)
    scale = HEAD_DIM ** -0.5

    def step(i, state_t):
        rows = pl.ds(pl.multiple_of(i * span, span), span)
        q = q_ref[rows, :]
        f = f_ref[rows, :]
        v16 = i_ref[rows, :].astype(BF16)

        ef = jnp.exp(-jnp.abs(f))
        inv = 1.0 / (1.0 + ef)
        log_sig = jnp.minimum(f, 0.0) - jnp.log(1.0 + ef)
        cc = log_1m_lb + log_sig
        log_f = jnp.maximum(log_lb, cc) + jnp.log(1.0 + jnp.exp(-jnp.abs(log_lb - cc)))
        k = one_m_lb * jnp.where(f >= 0.0, ef * inv, inv)
        qs = _silu(q) * scale
        b = _chunk_cumsum(log_f, row)

        q_in, k_stack, v_stack, q_out, k_out, decay_last = [], [], [], [], [], []
        for j in range(n_par):
            lo = j * CHUNK
            bj = b[lo:lo + CHUNK]
            kj = k[lo:lo + CHUNK]
            qj = qs[lo:lo + CHUNK]
            refs = [jnp.zeros((1, HEAD_DIM), F32)] + [bj[SUB * s - 1:SUB * s] for s in range(1, N_SUB)]
            ref_rows = jnp.concatenate([jnp.broadcast_to(r, (SUB, HEAD_DIM)) for r in refs], axis=0)
            q_in.append((qj * jnp.exp(bj - ref_rows)).astype(BF16))
            k_stack.append(jnp.concatenate(
                [kj * jnp.exp(jnp.minimum(r - bj, MAX_EXP_ARG)) for r in refs], axis=0).astype(BF16))
            v_stack.append(jnp.concatenate([v16[lo:lo + CHUNK]] * N_SUB, axis=0))
            q_out.append((qj * jnp.exp(bj)).astype(BF16))
            b_last = bj[CHUNK - 1:CHUNK]
            k_out.append((kj * jnp.exp(b_last - bj)).astype(BF16))
            decay_last.append(jnp.exp(b_last))

        scores = _bdot_nt(_stack(q_in), _stack(k_stack))
        scores = jnp.where(valid, scores, 0.0).astype(BF16)
        o = _bdot(scores, _stack(v_stack))
        d_state = _bdot_tn(v16.reshape(n_par, CHUNK, HEAD_DIM), _stack(k_out))
        states = [state_t]
        for j in range(n_par):
            states.append(states[j] * decay_last[j] + d_state[j])
        o = o + _bdot_nt(_stack(q_out), _stack([s.astype(BF16) for s in states[:n_par]]))

        o = _rms(o.reshape(span, HEAD_DIM), hn) * _silu(z_ref[rows, :])
        o_ref[rows, :] = o.astype(o_ref.dtype)
        return states[n_par]

    lax.fori_loop(0, n_chunks // n_par, step, jnp.zeros((HEAD_DIM, HEAD_DIM), F32))


def hgrn_mixer(proj, lb_param, head_norm, *, batch, seq, layer):
    heads = lb_param.shape[1] // HEAD_DIM
    col = lambda off: pl.BlockSpec((seq, HEAD_DIM), lambda b, h: (b, off * heads + h))
    return pl.pallas_call(
        functools.partial(_hgrn_body, layer=layer, n_chunks=seq // CHUNK),
        grid=(batch, heads),
        in_specs=[pl.BlockSpec((DEPTH, HEAD_DIM), lambda b, h: (0, h)),
                  pl.BlockSpec((1, HEAD_DIM), lambda b, h: (0, 0)),
                  col(0), col(1), col(2), col(3)],
        out_specs=pl.BlockSpec((seq, HEAD_DIM), lambda b, h: (b, h)),
        out_shape=jax.ShapeDtypeStruct((batch * seq, heads * HEAD_DIM), BF16),
        compiler_params=_params("parallel", "parallel"),
        name="hgrn_gla_l%d" % layer,
    )(lb_param, head_norm.reshape(1, HEAD_DIM), proj, proj, proj, proj)


def _rope_tables(pos_col):
    lane = lax.broadcasted_iota(jnp.int32, (1, HEAD_DIM), 1)
    j = (lane % (2 * ROPE_HALF)).astype(F32)
    inv_freq = jnp.exp(j * (-2.0 * math.log(ROPE_THETA) / (2 * ROPE_HALF)))
    live = (lane % (2 * ROPE_HALF)) < ROPE_HALF
    ang = pos_col * inv_freq
    cos = jnp.where(live, jnp.cos(ang), 0.0)
    sin = jnp.where(live, jnp.sin(ang), 0.0)
    sin = jnp.where(lane < 2 * ROPE_HALF, -sin, sin)
    return cos, sin


def _apply_rope(t, cos, sin):
    return t * cos + pltpu.roll(t, 2 * ROPE_HALF, axis=1) * sin


def _rope_prep_body(pos_ref, kr_ref, cos_ref, sin_ref, k_ref):
    cos, sin = _rope_tables(pos_ref[...].astype(F32))
    cos_ref[...] = cos
    sin_ref[...] = sin
    k_ref[...] = _apply_rope(kr_ref[...], cos, sin).astype(BF16)


def rope_prep(pos_col, proj, kr_block, *, tm):
    t = pos_col.shape[0]
    tm = min(tm, t)
    tab = pl.BlockSpec((tm, HEAD_DIM), lambda i: (i, 0))
    return pl.pallas_call(
        _rope_prep_body,
        grid=(t // tm,),
        in_specs=[pl.BlockSpec((tm, 1), lambda i: (i, 0)),
                  pl.BlockSpec((tm, HEAD_DIM), lambda i: (i, kr_block))],
        out_specs=[tab, tab, tab],
        out_shape=[jax.ShapeDtypeStruct((t, HEAD_DIM), F32),
                   jax.ShapeDtypeStruct((t, HEAD_DIM), F32),
                   jax.ShapeDtypeStruct((t, HEAD_DIM), BF16)],
        compiler_params=_params("parallel"),
        name="mla_rope_prep",
    )(pos_col, proj)


def _mla_attn_body(q_ref, cos_ref, sin_ref, kv_ref, kr_ref, z_ref, o_ref, *, tq, tk, n_heads):
    qi = pl.program_id(2)
    cos = cos_ref[...]
    sin = sin_ref[...]
    q_all = q_ref[...]
    q_cat = []
    for a in range(n_heads):
        q = q_all[:, 2 * a * HEAD_DIM:2 * (a + 1) * HEAD_DIM]
        q_rope = _apply_rope(q[:, HEAD_DIM:], cos, sin)
        q_cat.append((jnp.concatenate([q[:, :HEAD_DIM], q_rope], axis=1) * MLA_SCALE).astype(BF16))
    q_cat = _stack(q_cat)
    ones = jnp.ones((tk, HEAD_DIM), BF16)
    q_chunk = (qi * tq + lax.broadcasted_iota(jnp.int32, (tq, tk), 0)) // CHUNK
    k_in_block = lax.broadcasted_iota(jnp.int32, (tq, tk), 1)

    def block(j, carry, masked):
        m, acc = carry
        rows = pl.ds(pl.multiple_of(j * tk, tk), tk)
        kr = kr_ref[rows, :]
        k_cat, v_ext = [], []
        for a in range(n_heads):
            lo = 2 * a * HEAD_DIM
            k_cat.append(jnp.concatenate([kv_ref[rows, lo:lo + HEAD_DIM], kr], axis=1))
            v_ext.append(jnp.concatenate([kv_ref[rows, lo + HEAD_DIM:lo + 2 * HEAD_DIM], ones], axis=1))
        s = _bdot_nt(q_cat, _stack(k_cat))
        if masked:
            s = jnp.where((j * tk + k_in_block) // CHUNK <= q_chunk, s, -jnp.inf)
        m_new = jnp.maximum(m, jnp.max(s, axis=-1, keepdims=True))
        p = jnp.exp(s - m_new).astype(BF16)
        acc = jnp.exp(m - m_new) * acc + _bdot(p, _stack(v_ext))
        return m_new, acc

    init = (jnp.full((n_heads, tq, 1), -jnp.inf, F32), jnp.zeros((n_heads, tq, 2 * HEAD_DIM), F32))
    n_full = (qi * tq) // tk
    carry = lax.fori_loop(0, n_full, lambda j, c: block(j, c, False), init)
    _, acc = block(n_full, carry, True)
    z = z_ref[...]
    outs = [acc[a, :, :HEAD_DIM] / acc[a, :, HEAD_DIM:] * _silu(z[:, a * HEAD_DIM:(a + 1) * HEAD_DIM])
            for a in range(n_heads)]
    o_ref[...] = jnp.concatenate(outs, axis=1).astype(o_ref.dtype)


def mla_attention(q_up, cos, sin, kv_up, k_rope, proj, z_col0, *, batch, seq, heads):
    tq, tk, hps = min(MLA_TQ, seq), min(MLA_TK, seq), MLA_HEADS_PER_STEP
    assert seq % tk == 0 and tk % tq == 0 and tq % CHUNK == 0 and heads % hps == 0
    assert z_col0 % (hps * HEAD_DIM) == 0
    nq = seq // tq
    z_block0 = z_col0 // (hps * HEAD_DIM)
    return pl.pallas_call(
        functools.partial(_mla_attn_body, tq=tq, tk=tk, n_heads=hps),
        grid=(batch, heads // hps, nq),
        in_specs=[pl.BlockSpec((tq, hps * 2 * HEAD_DIM), lambda b, h, i: (b * nq + i, h)),
                  pl.BlockSpec((tq, HEAD_DIM), lambda b, h, i: (b * nq + i, 0)),
                  pl.BlockSpec((tq, HEAD_DIM), lambda b, h, i: (b * nq + i, 0)),
                  pl.BlockSpec((seq, hps * 2 * HEAD_DIM), lambda b, h, i: (b, h)),
                  pl.BlockSpec((seq, HEAD_DIM), lambda b, h, i: (b, 0)),
                  pl.BlockSpec((tq, hps * HEAD_DIM), lambda b, h, i: (b * nq + i, z_block0 + h))],
        out_specs=pl.BlockSpec((tq, hps * HEAD_DIM), lambda b, h, i: (b * nq + i, h)),
        out_shape=jax.ShapeDtypeStruct((batch * seq, heads * HEAD_DIM), BF16),
        compiler_params=_params("parallel", "parallel", "arbitrary"),
        name="mla_attention",
    )(q_up, cos, sin, kv_up, k_rope, proj)


def _gdn_gates_body(ab_ref, alog_ref, dt_ref, o_ref, *, n_heads):
    ab = ab_ref[...]
    x = ab[:, :n_heads] + dt_ref[...]
    softplus = jnp.maximum(x, 0.0) + jnp.log(1.0 + jnp.exp(-jnp.abs(x)))
    g = -jnp.exp(alog_ref[...]) * softplus
    row = lax.broadcasted_iota(jnp.int32, g.shape, 0) % CHUNK
    o_ref[...] = jnp.concatenate([_chunk_cumsum(g, row), _sigmoid(ab[:, n_heads:])], axis=1)


def gdn_gates(ab, a_log, dt_bias, *, batch, seq):
    n_heads = a_log.shape[0]
    vec = pl.BlockSpec((1, n_heads), lambda b: (0, 0))
    return pl.pallas_call(
        functools.partial(_gdn_gates_body, n_heads=n_heads),
        grid=(batch,),
        in_specs=[pl.BlockSpec((seq, 2 * n_heads), lambda b: (b, 0)), vec, vec],
        out_specs=pl.BlockSpec((seq, 2 * n_heads), lambda b: (b, 0)),
        out_shape=jax.ShapeDtypeStruct((batch * seq, 2 * n_heads), F32),
        compiler_params=_params("parallel"),
        name="gdn_gates",
    )(ab, a_log.reshape(1, n_heads), dt_bias.reshape(1, n_heads))


def _block_diag2(x, left):
    return jnp.concatenate([jnp.where(left, x, 0.0), jnp.where(left, 0.0, x)], axis=1)


def _tri_inverse_pair(low, eye, left):
    m = -low
    s = eye + m
    p = _bdot(m.astype(BF16), _block_diag2(m, left).astype(BF16))
    n = 2
    while 2 * n < CHUNK:
        rhs = jnp.concatenate([_block_diag2(p, left), _block_diag2(s, left)], axis=2).astype(BF16)
        out = _bdot(p.astype(BF16), rhs)
        p = out[:, :, :2 * CHUNK]
        s = s + out[:, :, 2 * CHUNK:]
        n *= 2
    return s + _bdot(p.astype(BF16), _block_diag2(s, left).astype(BF16))


def _gdn_body(cw_ref, hn_ref, q_ref, k_ref, v_ref, z_ref, gcol_ref, grow_ref, o_ref,
              at_s, low_s, rhs_s, qe_s, ko_s, inv_s, lhs_s, n_s, op_s, *, n_chunks):
    cw = cw_ref[...]
    hn = hn_ref[...]
    row = lax.broadcasted_iota(jnp.int32, (CHUNK, 2 * CHUNK), 0)
    lane = lax.broadcasted_iota(jnp.int32, (CHUNK, 2 * CHUNK), 1)
    left = lane < CHUNK
    col = lane % CHUNK
    tri = col <= row
    strict = col < row
    eye = (col == row).astype(F32)
    zeros_wu = jnp.zeros((CHUNK, 2 * HEAD_DIM), BF16)

    def conv_silu(x_ref, c, w):
        start = pl.multiple_of(c * CHUNK, CHUNK)
        cur = x_ref[pl.ds(start, CHUNK), :]
        prev_start = pl.multiple_of(jnp.maximum(start - 8, 0), 8)
        prev = x_ref[pl.ds(prev_start, 8), :] * (c > 0).astype(F32)
        ext = jnp.concatenate([prev, cur], axis=0)
        y = cur * w[CONV_WIDTH - 1:CONV_WIDTH]
        for d in range(1, CONV_WIDTH):
            y = y + ext[8 - d:8 - d + CHUNK] * w[CONV_WIDTH - 1 - d:CONV_WIDTH - d]
        return _silu(y)

    def l2n(x):
        return x * lax.rsqrt(jnp.sum(x * x, axis=-1, keepdims=True) + EPS)

    n_prep = math.gcd(n_chunks, GDN_PREP_UNROLL)
    n_inv = math.gcd(n_chunks, GDN_INVERT_UNROLL)

    def prepare(i, _):
        lhs, rhs, kept = [], [], []
        for j in range(n_prep):
            c = i * n_prep + j
            q = l2n(conv_silu(q_ref, c, cw[:, :HEAD_DIM])) * (HEAD_DIM ** -0.5)
            k = l2n(conv_silu(k_ref, c, cw[:, HEAD_DIM:2 * HEAD_DIM]))
            v = conv_silu(v_ref, c, cw[:, 2 * HEAD_DIM:])
            q16 = q.astype(BF16)
            k16 = k.astype(BF16)
            lhs.append(jnp.concatenate([q16, k16], axis=0))
            rhs.append(jnp.concatenate([k16, k16], axis=0))
            kept.append((c, q, k, v))
        qk_kk = _bdot_nt(_stack(lhs), _stack(rhs))
        for j, (c, q, k, v) in enumerate(kept):
            rows = pl.ds(pl.multiple_of(c * CHUNK, CHUNK), CHUNK)
            gcol = gcol_ref[0, 0, rows, :]
            g_pair = jnp.where(left, gcol[:, 0:1], gcol[:, 1:2])
            beta_pair = jnp.where(left, gcol[:, 2:3], gcol[:, 3:4])
            g_row = grow_ref[0, 0, pl.ds(c, 1), :]
            decay = jnp.exp(jnp.minimum(g_pair - g_row, 0.0))
            at_s[rows, :] = jnp.where(tri, qk_kk[j, :CHUNK] * decay, 0.0).astype(BF16)
            low_s[rows, :] = jnp.where(strict, qk_kk[j, CHUNK:] * decay, 0.0) * beta_pair
            for hh in range(2):
                gc = gcol[:, hh:hh + 1]
                beta = gcol[:, 2 + hh:3 + hh]
                e_g = jnp.exp(gc)
                rhs_s[hh, rows, :] = jnp.concatenate([v[:, hh * HEAD_DIM:(hh + 1) * HEAD_DIM] * beta,
                                                      k * (beta * e_g)], axis=1).astype(BF16)
                qe_s[hh, rows, :] = q * e_g
                ko_s[hh, rows, :] = (k * jnp.exp(gc[CHUNK - 1:CHUNK] - gc)).astype(BF16)
        return 0

    def invert(i, _):
        rows = pl.ds(pl.multiple_of(i * (n_inv * CHUNK), n_inv * CHUNK), n_inv * CHUNK)
        low = low_s[rows, :].reshape(n_inv, CHUNK, 2 * CHUNK)
        inv = _tri_inverse_pair(low, eye, left)
        inv_s[rows, :] = inv.reshape(n_inv * CHUNK, 2 * CHUNK).astype(BF16)
        return 0

    def block_diag_wide(x0, x1):
        return jnp.concatenate([jnp.concatenate([x0, zeros_wu], axis=1),
                                jnp.concatenate([zeros_wu, x1], axis=1)], axis=0)

    def combine(i, _):
        span = n_prep * CHUNK
        rows_all = pl.ds(pl.multiple_of(i * span, span), span)
        chunk_rows = [pl.ds(pl.multiple_of((i * n_prep + j) * CHUNK, CHUNK), CHUNK) for j in range(n_prep)]
        rhs_bd = _stack([block_diag_wide(rhs_s[0, r, :], rhs_s[1, r, :]) for r in chunk_rows])
        inv = inv_s[rows_all, :].reshape(n_prep, CHUNK, 2 * CHUNK)
        uw = _bdot(inv, rhs_bd).astype(BF16)
        uw_bd = _stack([block_diag_wide(uw[j, :, :2 * HEAD_DIM], uw[j, :, 2 * HEAD_DIM:])
                        for j in range(n_prep)])
        attn = at_s[rows_all, :].reshape(n_prep, CHUNK, 2 * CHUNK)
        a_uw = _bdot(attn, uw_bd)
        k_out = _stack([ko_s[hh, r, :] for r in chunk_rows for hh in range(2)])
        uw_h = _stack([uw[j, :, 2 * hh * HEAD_DIM:2 * (hh + 1) * HEAD_DIM]
                       for j in range(n_prep) for hh in range(2)])
        kt_uw = _bdot_tn(k_out, uw_h)
        for j in range(n_prep):
            c = i * n_prep + j
            base = pl.multiple_of(c * (HEAD_DIM + CHUNK), HEAD_DIM + CHUNK)
            n_rows = pl.ds(pl.multiple_of(c * HEAD_DIM, HEAD_DIM), HEAD_DIM)
            for hh in range(2):
                lo = 2 * hh * HEAD_DIM
                kt = kt_uw[2 * j + hh]
                q_eff = qe_s[hh, chunk_rows[j], :] - a_uw[j, :, lo + HEAD_DIM:lo + 2 * HEAD_DIM]
                lhs_s[hh, pl.ds(base, HEAD_DIM), :] = kt[:, HEAD_DIM:].astype(BF16)
                lhs_s[hh, pl.ds(base + HEAD_DIM, CHUNK), :] = q_eff.astype(BF16)
                n_s[hh, n_rows, :] = kt[:, :HEAD_DIM]
                op_s[hh, chunk_rows[j], :] = a_uw[j, :, lo:lo + HEAD_DIM]
        return 0

    lax.fori_loop(0, n_chunks // n_prep, prepare, 0)
    lax.fori_loop(0, n_chunks // n_inv, invert, 0)
    lax.fori_loop(0, n_chunks // n_prep, combine, 0)

    def recur(c, states):
        rows = pl.ds(pl.multiple_of(c * CHUNK, CHUNK), CHUNK)
        z = z_ref[rows, :]
        base = pl.multiple_of(c * (HEAD_DIM + CHUNK), HEAD_DIM + CHUNK)
        n_rows = pl.ds(pl.multiple_of(c * HEAD_DIM, HEAD_DIM), HEAD_DIM)
        pr = _bdot(lhs_s[:, pl.ds(base, HEAD_DIM + CHUNK), :],
                   _stack([st.astype(BF16) for st in states]))
        outs = []
        new_states = []
        for hh in range(2):
            o = pr[hh, HEAD_DIM:] + op_s[hh, rows, :]
            g_last = gcol_ref[0, 0, pl.ds(c * CHUNK + CHUNK - 1, 1), hh:hh + 1]
            new_states.append(states[hh] * jnp.exp(g_last) + (n_s[hh, n_rows, :] - pr[hh, :HEAD_DIM]))
            outs.append(_rms(o, hn) * _silu(z[:, hh * HEAD_DIM:(hh + 1) * HEAD_DIM]))
        o_ref[rows, :] = jnp.concatenate(outs, axis=1).astype(o_ref.dtype)
        return tuple(new_states)

    zero = jnp.zeros((HEAD_DIM, HEAD_DIM), F32)
    lax.fori_loop(0, n_chunks, recur, (zero, zero), unroll=math.gcd(n_chunks, GDN_RECUR_UNROLL))


def gdn_mixer(proj, conv_w, head_norm, gcol, grow, *, batch, seq, qk_heads):
    nc = seq // CHUNK
    key_w = qk_heads * HEAD_DIM
    cw = jnp.concatenate([
        conv_w[:, :key_w].reshape(CONV_WIDTH, qk_heads, HEAD_DIM),
        conv_w[:, key_w:2 * key_w].reshape(CONV_WIDTH, qk_heads, HEAD_DIM),
        conv_w[:, 2 * key_w:].reshape(CONV_WIDTH, qk_heads, 2 * HEAD_DIM)], axis=2)
    cw = cw.transpose(1, 0, 2)
    v_block0 = 2 * qk_heads // 2
    z_block0 = v_block0 + qk_heads
    assert seq % CHUNK == 0 and gcol.shape == (batch, qk_heads, seq, 4)
    assert grow.shape == (batch, qk_heads, nc, 2 * CHUNK)
    return pl.pallas_call(
        functools.partial(_gdn_body, n_chunks=nc),
        grid=(batch, qk_heads),
        in_specs=[pl.BlockSpec((None, CONV_WIDTH, 4 * HEAD_DIM), lambda b, g: (g, 0, 0)),
                  pl.BlockSpec((1, HEAD_DIM), lambda b, g: (0, 0)),
                  pl.BlockSpec((seq, HEAD_DIM), lambda b, g: (b, g)),
                  pl.BlockSpec((seq, HEAD_DIM), lambda b, g: (b, qk_heads + g)),
                  pl.BlockSpec((seq, 2 * HEAD_DIM), lambda b, g: (b, v_block0 + g)),
                  pl.BlockSpec((seq, 2 * HEAD_DIM), lambda b, g: (b, z_block0 + g)),
                  pl.BlockSpec((1, 1, seq, 4), lambda b, g: (b, g, 0, 0)),
                  pl.BlockSpec((1, 1, nc, 2 * CHUNK), lambda b, g: (b, g, 0, 0))],
        out_specs=pl.BlockSpec((seq, 2 * HEAD_DIM), lambda b, g: (b, g)),
        out_shape=jax.ShapeDtypeStruct((batch * seq, 2 * qk_heads * HEAD_DIM), BF16),
        scratch_shapes=[pltpu.VMEM((seq, 2 * CHUNK), BF16),
                        pltpu.VMEM((seq, 2 * CHUNK), F32),
                        pltpu.VMEM((2, seq, 2 * HEAD_DIM), BF16),
                        pltpu.VMEM((2, seq, HEAD_DIM), F32),
                        pltpu.VMEM((2, seq, HEAD_DIM), BF16),
                        pltpu.VMEM((seq, 2 * CHUNK), BF16),
                        pltpu.VMEM((2, nc * (HEAD_DIM + CHUNK), HEAD_DIM), BF16),
                        pltpu.VMEM((2, nc * HEAD_DIM, HEAD_DIM), F32),
                        pltpu.VMEM((2, seq, HEAD_DIM), F32)],
        compiler_params=_params("parallel", "parallel"),
        name="gdn_delta_rule",
    )(cw, head_norm.reshape(1, HEAD_DIM), proj, proj, proj, proj, gcol, grow)


def _hgrn_layer(h, lb_param, pre, post, w_in, head_norm, w_out, *, batch, seq, layer):
    proj = norm_matmul(h, 0, pre, w_in.astype(BF16), tm=1024, tn=1024, out_dtype=F32,
                       name="hgrn_in_proj_l%d" % layer)
    gated = hgrn_mixer(proj, lb_param, head_norm, batch=batch, seq=seq, layer=layer)
    return out_proj_residual(gated, w_out.astype(BF16), post, h, tm=512,
                             name="hgrn_out_proj_l%d" % layer)


def _pad_rope_cols(w):
    zeros = jnp.zeros(w.shape[:-1] + (ROPE_HALF,), w.dtype)
    return jnp.concatenate([w[..., :ROPE_HALF], zeros, w[..., ROPE_HALF:], zeros], axis=-1)


def _mla_layer(h, positions, pre, post, w_in, q_norm, kv_norm, w_uq, w_ukv, w_out, *, batch, seq):
    d_model = h.shape[1]
    q_rank = q_norm.shape[0]
    kv_rank = kv_norm.shape[0]
    heads = w_out.shape[0] // HEAD_DIM
    lat = q_rank + kv_rank
    gate_w = heads * HEAD_DIM
    w_in_p = jnp.concatenate([w_in[:, :lat], w_in[:, lat + 2 * ROPE_HALF:],
                              _pad_rope_cols(w_in[:, lat:lat + 2 * ROPE_HALF])], axis=1).astype(BF16)
    proj = norm_matmul(h, 0, pre, w_in_p, tm=1024, tn=640, out_dtype=F32, name="mla_in_proj")
    w_uq_h = w_uq.reshape(q_rank, heads, HEAD_DIM + 2 * ROPE_HALF)
    w_uq_p = jnp.concatenate([w_uq_h[..., :HEAD_DIM], _pad_rope_cols(w_uq_h[..., HEAD_DIM:])],
                             axis=-1).reshape(q_rank, heads * 2 * HEAD_DIM).astype(BF16)
    q_up = norm_matmul(proj, 0, q_norm, w_uq_p, tm=1024, tn=1024, out_dtype=F32, name="mla_q_up")
    kv_up = norm_matmul(proj, kv_rank // q_rank, kv_norm, w_ukv.astype(BF16), tm=1024, tn=1024,
                        out_dtype=BF16, name="mla_kv_up")
    pos_col = positions.reshape(batch * seq, 1)
    cos, sin, k_rope = rope_prep(pos_col, proj, (lat + gate_w) // HEAD_DIM, tm=1024)
    gated = mla_attention(q_up, cos, sin, kv_up, k_rope, proj, lat, batch=batch, seq=seq, heads=heads)
    del d_model
    return out_proj_residual(gated, w_out.astype(BF16), post, h, tm=512, name="mla_out_proj")


def _gdn_layer(h, pre, post, w_in, conv_w, a_log, dt_bias, head_norm, w_out, *, batch, seq):
    v_heads = a_log.shape[0]
    qk_heads = v_heads // 2
    main = 2 * qk_heads * HEAD_DIM + 2 * v_heads * HEAD_DIM
    proj = norm_matmul(h, 0, pre, w_in[:, :main].astype(BF16), tm=1024, tn=1024, out_dtype=F32,
                       name="gdn_in_proj")
    w_ab = jnp.pad(w_in[:, main:], ((0, 0), (0, HEAD_DIM - 2 * v_heads))).astype(BF16)
    ab = norm_matmul(h, 0, pre, w_ab, tm=1024, tn=HEAD_DIM, out_dtype=F32,
                     name="gdn_gate_proj")[:, :2 * v_heads]
    nc = seq // CHUNK
    col = gdn_gates(ab, a_log, dt_bias, batch=batch, seq=seq)
    gcol = col.reshape(batch, seq, 2, qk_heads, 2).transpose(0, 3, 1, 2, 4)
    gcol = gcol.reshape(batch, qk_heads, seq, 4)
    grow = col[:, :v_heads].reshape(batch, nc, CHUNK, qk_heads, 2).transpose(0, 3, 1, 4, 2)
    grow = grow.reshape(batch, qk_heads, nc, 2 * CHUNK)
    gated = gdn_mixer(proj, conv_w, head_norm, gcol, grow, batch=batch, seq=seq, qk_heads=qk_heads)
    return out_proj_residual(gated, w_out.astype(BF16), post, h, tm=512, name="gdn_out_proj")


def kernel(x, positions, hgrn_lb, l0_pre_norm, l0_post_norm, l0_w_in, l0_head_norm, l0_w_out, l1_pre_norm, l1_post_norm, l1_w_in, l1_q_norm, l1_kv_norm, l1_w_uq, l1_w_ukv, l1_w_out, l2_pre_norm, l2_post_norm, l2_w_in, l2_conv_w, l2_a_log, l2_dt_bias, l2_head_norm, l2_w_out, l3_pre_norm, l3_post_norm, l3_w_in, l3_head_norm, l3_w_out):
    batch, seq, d_model = x.shape
    h = x.reshape(batch * seq, d_model)
    h = _hgrn_layer(h, hgrn_lb, l0_pre_norm, l0_post_norm, l0_w_in, l0_head_norm, l0_w_out,
                    batch=batch, seq=seq, layer=0)
    h = _mla_layer(h, positions, l1_pre_norm, l1_post_norm, l1_w_in, l1_q_norm, l1_kv_norm,
                   l1_w_uq, l1_w_ukv, l1_w_out, batch=batch, seq=seq)
    h = _gdn_layer(h, l2_pre_norm, l2_post_norm, l2_w_in, l2_conv_w, l2_a_log, l2_dt_bias,
                   l2_head_norm, l2_w_out, batch=batch, seq=seq)
    h = _hgrn_layer(h, hgrn_lb, l3_pre_norm, l3_post_norm, l3_w_in, l3_head_norm, l3_w_out,
                    batch=batch, seq=seq, layer=3)
    return h.reshape(batch, seq, d_model)
```

```python
import functools
import math

import jax
import jax.numpy as jnp
from jax import lax
from jax.experimental import pallas as pl
from jax.experimental.pallas import tpu as pltpu

F32 = jnp.float32
BF16 = jnp.bfloat16

EPS = 1e-6
CHUNK = 64
SUB = 16
N_SUB = CHUNK // SUB
HEAD_DIM = 128
DEPTH = 4
ROPE_HALF = 32
ROPE_THETA = 10000.0
MLA_SCALE = (128 + 64) ** -0.5
CONV_WIDTH = 4
CONV_PAD = 8
MAX_EXP_ARG = 80.0
VMEM_LIMIT_BYTES = 56 * 1024 * 1024
HGRN_UNROLL = 8
GDN_PREP_UNROLL = 4
GDN_INVERT_UNROLL = 16
GDN_COMBINE_UNROLL = 8
GDN_RECUR_UNROLL = 8
MLA_TQ = 256
MLA_TK = 512
MLA_HEADS_PER_STEP = 2

NT_DIMS = (((1,), (1,)), ((), ()))
TN_DIMS = (((0,), (0,)), ((), ()))


def _params(*semantics):
    return pltpu.CompilerParams(dimension_semantics=semantics,
                                vmem_limit_bytes=VMEM_LIMIT_BYTES)


def _sigmoid(x):
    return 1.0 / (1.0 + jnp.exp(-x))


def _silu(x):
    return x * _sigmoid(x)


def _rms(x, gain):
    ms = jnp.mean(x * x, axis=-1, keepdims=True)
    return x * lax.rsqrt(ms + EPS) * gain


def _dot(a, b):
    return jnp.dot(a, b, preferred_element_type=F32)


def _dot_nt(a, b):
    return lax.dot_general(a, b, NT_DIMS, preferred_element_type=F32)


def _dot_tn(a, b):
    return lax.dot_general(a, b, TN_DIMS, preferred_element_type=F32)


def _stack(xs):
    m, n = xs[0].shape
    return jnp.concatenate(xs, axis=0).reshape(len(xs), m, n)


def _bdot(a, b):
    return lax.dot_general(a, b, (((2,), (1,)), ((0,), (0,))), preferred_element_type=F32)


def _bdot_nt(a, b):
    return lax.dot_general(a, b, (((2,), (2,)), ((0,), (0,))), preferred_element_type=F32)


def _bdot_tn(a, b):
    return lax.dot_general(a, b, (((1,), (1,)), ((0,), (0,))), preferred_element_type=F32)


def _chunk_cumsum(x, row):
    shift = 1
    while shift < CHUNK:
        x = x + jnp.where(row >= shift, pltpu.roll(x, shift, axis=0), 0.0)
        shift *= 2
    return x


def _norm_matmul_body(x_ref, g_ref, w_ref, o_ref, xn_ref):
    @pl.when(pl.program_id(1) == 0)
    def _():
        xn_ref[...] = _rms(x_ref[...], g_ref[...]).astype(BF16)

    o_ref[...] = _dot(xn_ref[...], w_ref[...]).astype(o_ref.dtype)


def norm_matmul(x, col_block, gain, w, *, tm, tn, out_dtype, name):
    t = x.shape[0]
    k, n = w.shape
    tm, tn = min(tm, t), min(tn, n)
    return pl.pallas_call(
        _norm_matmul_body,
        grid=(t // tm, n // tn),
        in_specs=[pl.BlockSpec((tm, k), lambda i, j: (i, col_block)),
                  pl.BlockSpec((1, k), lambda i, j: (0, 0)),
                  pl.BlockSpec((k, tn), lambda i, j: (0, j))],
        out_specs=pl.BlockSpec((tm, tn), lambda i, j: (i, j)),
        out_shape=jax.ShapeDtypeStruct((t, n), out_dtype),
        scratch_shapes=[pltpu.VMEM((tm, k), BF16)],
        compiler_params=_params("parallel", "arbitrary"),
        name=name,
    )(x, gain.reshape(1, k), w)


def _out_proj_body(g_ref, w_ref, p_ref, h_ref, o_ref):
    y = _dot(g_ref[...], w_ref[...])
    o_ref[...] = h_ref[...] + _rms(y, p_ref[...])


def out_proj_residual(g, w, post, h, *, tm, name):
    t, k = g.shape
    d = w.shape[1]
    tm = min(tm, t)
    return pl.pallas_call(
        _out_proj_body,
        grid=(t // tm,),
        in_specs=[pl.BlockSpec((tm, k), lambda i: (i, 0)),
                  pl.BlockSpec((k, d), lambda i: (0, 0), pipeline_mode=pl.Buffered(1)),
                  pl.BlockSpec((1, d), lambda i: (0, 0)),
                  pl.BlockSpec((tm, d), lambda i: (i, 0))],
        out_specs=pl.BlockSpec((tm, d), lambda i: (i, 0)),
        out_shape=jax.ShapeDtypeStruct((t, d), F32),
        compiler_params=_params("parallel"),
        name=name,
    )(g, w, post.reshape(1, d), h)


def _hgrn_body(lbp_ref, hn_ref, q_ref, f_ref, i_ref, z_ref, o_ref, *, layer, n_chunks):
    p = lbp_ref[...]
    e = jnp.exp(p - jnp.max(p, axis=0, keepdims=True))
    sm = e / jnp.sum(e, axis=0, keepdims=True)
    lb = jnp.zeros((1, HEAD_DIM), F32)
    for j in range(1, layer + 1):
        lb = lb + sm[j:j + 1]
    log_lb = jnp.log(lb)
    log_1m_lb = jnp.log(1.0 - lb)
    one_m_lb = 1.0 - lb
    hn = hn_ref[...]

    n_par = math.gcd(n_chunks, HGRN_UNROLL)
    span = n_par * CHUNK
    row = lax.broadcasted_iota(jnp.int32, (span, HEAD_DIM), 0) % CHUNK
    t_idx = lax.broadcasted_iota(jnp.int32, (CHUNK, N_SUB * CHUNK), 0)
    c_idx = lax.broadcasted_iota(jnp.int32, (CHUNK, N_SUB * CHUNK), 1)
    valid = ((c_idx // CHUNK) == (t_idx // SUB)) & ((c_idx % CHUNK) <= t_idx)
    scale = HEAD_DIM ** -0.5

    def step(i, state_t):
        rows = pl.ds(pl.multiple_of(i * span, span), span)
        q = q_ref[rows, :]
        f = f_ref[rows, :]
        v16 = i_ref[rows, :].astype(BF16)

        ef = jnp.exp(-jnp.abs(f))
        inv = 1.0 / (1.0 + ef)
        log_sig = jnp.minimum(f, 0.0) - jnp.log(1.0 + ef)
        cc = log_1m_lb + log_sig
        log_f = jnp.maximum(log_lb, cc) + jnp.log(1.0 + jnp.exp(-jnp.abs(log_lb - cc)))
        k = one_m_lb * jnp.where(f >= 0.0, ef * inv, inv)
        qs = _silu(q) * scale
        b = _chunk_cumsum(log_f, row)

        q_in, k_stack, v_stack, q_out, k_out, decay_last = [], [], [], [], [], []
        for j in range(n_par):
            lo = j * CHUNK
            bj = b[lo:lo + CHUNK]
            kj = k[lo:lo + CHUNK]
            qj = qs[lo:lo + CHUNK]
            refs = [jnp.zeros((1, HEAD_DIM), F32)] + [bj[SUB * s - 1:SUB * s] for s in range(1, N_SUB)]
            ref_rows = jnp.concatenate([jnp.broadcast_to(r, (SUB, HEAD_DIM)) for r in refs], axis=0)
            q_in.append((qj * jnp.exp(bj - ref_rows)).astype(BF16))
            k_stack.append(jnp.concatenate(
                [kj * jnp.exp(jnp.minimum(r - bj, MAX_EXP_ARG)) for r in refs], axis=0).astype(BF16))
            v_stack.append(jnp.concatenate([v16[lo:lo + CHUNK]] * N_SUB, axis=0))
            q_out.append((qj * jnp.exp(bj)).astype(BF16))
            b_last = bj[CHUNK - 1:CHUNK]
            k_out.append((kj * jnp.exp(b_last - bj)).astype(BF16))
            decay_last.append(jnp.exp(b_last))

        scores = _bdot_nt(_stack(q_in), _stack(k_stack))
        scores = jnp.where(valid, scores, 0.0).astype(BF16)
        o = _bdot(scores, _stack(v_stack))
        d_state = _bdot_tn(v16.reshape(n_par, CHUNK, HEAD_DIM), _stack(k_out))
        states = [state_t]
        for j in range(n_par):
            states.append(states[j] * decay_last[j] + d_state[j])
        o = o + _bdot_nt(_stack(q_out), _stack([s.astype(BF16) for s in states[:n_par]]))

        o = _rms(o.reshape(span, HEAD_DIM), hn) * _silu(z_ref[rows, :])
        o_ref[rows, :] = o.astype(o_ref.dtype)
        return states[n_par]

    lax.fori_loop(0, n_chunks // n_par, step, jnp.zeros((HEAD_DIM, HEAD_DIM), F32))


def hgrn_mixer(proj, lb_param, head_norm, *, batch, seq, layer):
    heads = lb_param.shape[1] // HEAD_DIM
    col = lambda off: pl.BlockSpec((seq, HEAD_DIM), lambda b, h: (b, off * heads + h))
    return pl.pallas_call(
        functools.partial(_hgrn_body, layer=layer, n_chunks=seq // CHUNK),
        grid=(batch, heads),
        in_specs=[pl.BlockSpec((DEPTH, HEAD_DIM), lambda b, h: (0, h)),
                  pl.BlockSpec((1, HEAD_DIM), lambda b, h: (0, 0)),
                  col(0), col(1), col(2), col(3)],
        out_specs=pl.BlockSpec((seq, HEAD_DIM), lambda b, h: (b, h)),
        out_shape=jax.ShapeDtypeStruct((batch * seq, heads * HEAD_DIM), BF16),
        compiler_params=_params("parallel", "parallel"),
        name="hgrn_gla_l%d" % layer,
    )(lb_param, head_norm.reshape(1, HEAD_DIM), proj, proj, proj, proj)


def _rope_tables(pos_col):
    lane = lax.broadcasted_iota(jnp.int32, (1, HEAD_DIM), 1)
    j = (lane % (2 * ROPE_HALF)).astype(F32)
    inv_freq = jnp.exp(j * (-2.0 * math.log(ROPE_THETA) / (2 * ROPE_HALF)))
    live = (lane % (2 * ROPE_HALF)) < ROPE_HALF
    ang = pos_col * inv_freq
    cos = jnp.where(live, jnp.cos(ang), 0.0)
    sin = jnp.where(live, jnp.sin(ang), 0.0)
    sin = jnp.where(lane < 2 * ROPE_HALF, -sin, sin)
    return cos, sin


def _apply_rope(t, cos, sin):
    return t * cos + pltpu.roll(t, 2 * ROPE_HALF, axis=1) * sin


def _rope_prep_body(pos_ref, kr_ref, cos_ref, sin_ref, k_ref):
    cos, sin = _rope_tables(pos_ref[...].astype(F32))
    cos_ref[...] = cos
    sin_ref[...] = sin
    k_ref[...] = _apply_rope(kr_ref[...], cos, sin).astype(BF16)


def rope_prep(pos_col, proj, kr_block, *, tm):
    t = pos_col.shape[0]
    tm = min(tm, t)
    tab = pl.BlockSpec((tm, HEAD_DIM), lambda i: (i, 0))
    return pl.pallas_call(
        _rope_prep_body,
        grid=(t // tm,),
        in_specs=[pl.BlockSpec((tm, 1), lambda i: (i, 0)),
                  pl.BlockSpec((tm, HEAD_DIM), lambda i: (i, kr_block))],
        out_specs=[tab, tab, tab],
        out_shape=[jax.ShapeDtypeStruct((t, HEAD_DIM), F32),
                   jax.ShapeDtypeStruct((t, HEAD_DIM), F32),
                   jax.ShapeDtypeStruct((t, HEAD_DIM), BF16)],
        compiler_params=_params("parallel"),
        name="mla_rope_prep",
    )(pos_col, proj)


def _mla_attn_body(q_ref, cos_ref, sin_ref, kv_ref, kr_ref, z_ref, o_ref, *, tq, tk, n_heads):
    qi = pl.program_id(2)
    cos = cos_ref[...]
    sin = sin_ref[...]
    q_all = q_ref[...]
    q_cat = []
    for a in range(n_heads):
        q = q_all[:, 2 * a * HEAD_DIM:2 * (a + 1) * HEAD_DIM]
        q_rope = _apply_rope(q[:, HEAD_DIM:], cos, sin)
        q_cat.append((jnp.concatenate([q[:, :HEAD_DIM], q_rope], axis=1) * MLA_SCALE).astype(BF16))
    q_cat = _stack(q_cat)
    ones = jnp.ones((tk, HEAD_DIM), BF16)
    q_chunk = (qi * tq + lax.broadcasted_iota(jnp.int32, (tq, tk), 0)) // CHUNK
    k_in_block = lax.broadcasted_iota(jnp.int32, (tq, tk), 1)

    def block(j, carry, masked):
        m, acc = carry
        rows = pl.ds(pl.multiple_of(j * tk, tk), tk)
        kr = kr_ref[rows, :]
        k_cat, v_ext = [], []
        for a in range(n_heads):
            lo = 2 * a * HEAD_DIM
            k_cat.append(jnp.concatenate([kv_ref[rows, lo:lo + HEAD_DIM], kr], axis=1))
            v_ext.append(jnp.concatenate([kv_ref[rows, lo + HEAD_DIM:lo + 2 * HEAD_DIM], ones], axis=1))
        s = _bdot_nt(q_cat, _stack(k_cat))
        if masked:
            s = jnp.where((j * tk + k_in_block) // CHUNK <= q_chunk, s, -jnp.inf)
        m_new = jnp.maximum(m, jnp.max(s, axis=-1, keepdims=True))
        p = jnp.exp(s - m_new).astype(BF16)
        acc = jnp.exp(m - m_new) * acc + _bdot(p, _stack(v_ext))
        return m_new, acc

    init = (jnp.full((n_heads, tq, 1), -jnp.inf, F32), jnp.zeros((n_heads, tq, 2 * HEAD_DIM), F32))
    n_full = (qi * tq) // tk
    carry = lax.fori_loop(0, n_full, lambda j, c: block(j, c, False), init)
    _, acc = block(n_full, carry, True)
    z = z_ref[...]
    outs = [acc[a, :, :HEAD_DIM] / acc[a, :, HEAD_DIM:] * _silu(z[:, a * HEAD_DIM:(a + 1) * HEAD_DIM])
            for a in range(n_heads)]
    o_ref[...] = jnp.concatenate(outs, axis=1).astype(o_ref.dtype)


def mla_attention(q_up, cos, sin, kv_up, k_rope, proj, z_col0, *, batch, seq, heads):
    tq, tk, hps = min(MLA_TQ, seq), min(MLA_TK, seq), MLA_HEADS_PER_STEP
    assert seq % tk == 0 and tk % tq == 0 and tq % CHUNK == 0 and heads % hps == 0
    assert z_col0 % (hps * HEAD_DIM) == 0
    nq = seq // tq
    z_block0 = z_col0 // (hps * HEAD_DIM)
    return pl.pallas_call(
        functools.partial(_mla_attn_body, tq=tq, tk=tk, n_heads=hps),
        grid=(batch, heads // hps, nq),
        in_specs=[pl.BlockSpec((tq, hps * 2 * HEAD_DIM), lambda b, h, i: (b * nq + i, h)),
                  pl.BlockSpec((tq, HEAD_DIM), lambda b, h, i: (b * nq + i, 0)),
                  pl.BlockSpec((tq, HEAD_DIM), lambda b, h, i: (b * nq + i, 0)),
                  pl.BlockSpec((seq, hps * 2 * HEAD_DIM), lambda b, h, i: (b, h)),
                  pl.BlockSpec((seq, HEAD_DIM), lambda b, h, i: (b, 0)),
                  pl.BlockSpec((tq, hps * HEAD_DIM), lambda b, h, i: (b * nq + i, z_block0 + h))],
        out_specs=pl.BlockSpec((tq, hps * HEAD_DIM), lambda b, h, i: (b * nq + i, h)),
        out_shape=jax.ShapeDtypeStruct((batch * seq, heads * HEAD_DIM), BF16),
        compiler_params=_params("parallel", "parallel", "arbitrary"),
        name="mla_attention",
    )(q_up, cos, sin, kv_up, k_rope, proj)


def _gdn_gates_body(ab_ref, alog_ref, dt_ref, o_ref, *, n_heads):
    ab = ab_ref[...]
    x = ab[:, :n_heads] + dt_ref[...]
    softplus = jnp.maximum(x, 0.0) + jnp.log(1.0 + jnp.exp(-jnp.abs(x)))
    g = -jnp.exp(alog_ref[...]) * softplus
    row = lax.broadcasted_iota(jnp.int32, g.shape, 0) % CHUNK
    o_ref[...] = jnp.concatenate([_chunk_cumsum(g, row), _sigmoid(ab[:, n_heads:])], axis=1)


def gdn_gates(ab, a_log, dt_bias, *, batch, seq):
    n_heads = a_log.shape[0]
    vec = pl.BlockSpec((1, n_heads), lambda b: (0, 0))
    return pl.pallas_call(
        functools.partial(_gdn_gates_body, n_heads=n_heads),
        grid=(batch,),
        in_specs=[pl.BlockSpec((seq, 2 * n_heads), lambda b: (b, 0)), vec, vec],
        out_specs=pl.BlockSpec((seq, 2 * n_heads), lambda b: (b, 0)),
        out_shape=jax.ShapeDtypeStruct((batch * seq, 2 * n_heads), F32),
        compiler_params=_params("parallel"),
        name="gdn_gates",
    )(ab, a_log.reshape(1, n_heads), dt_bias.reshape(1, n_heads))


def _block_diag2(x, left):
    return jnp.concatenate([jnp.where(left, x, 0.0), jnp.where(left, 0.0, x)], axis=1)


def _tri_inverse_pair(low, eye, left):
    m = -low
    s = eye + m
    p = _bdot(m.astype(BF16), _block_diag2(m, left).astype(BF16))
    n = 2
    while 2 * n < CHUNK:
        rhs = jnp.concatenate([_block_diag2(p, left), _block_diag2(s, left)], axis=2).astype(BF16)
        out = _bdot(p.astype(BF16), rhs)
        p = out[:, :, :2 * CHUNK]
        s = s + out[:, :, 2 * CHUNK:]
        n *= 2
    return s + _bdot(p.astype(BF16), _block_diag2(s, left).astype(BF16))


def _gdn_body(cw_ref, hn_ref, q_ref, k_ref, v_ref, z_ref, gcol_ref, grow_ref, o_ref,
              at_s, low_s, rhs_s, qe_s, ko_s, inv_s, lhs_s, n_s, op_s, *, n_chunks):
    cw = cw_ref[...]
    hn = hn_ref[...]
    row = lax.broadcasted_iota(jnp.int32, (CHUNK, 2 * CHUNK), 0)
    lane = lax.broadcasted_iota(jnp.int32, (CHUNK, 2 * CHUNK), 1)
    left = lane < CHUNK
    col = lane % CHUNK
    tri = col <= row
    strict = col < row
    eye = (col == row).astype(F32)
    zeros_wu = jnp.zeros((CHUNK, 2 * HEAD_DIM), BF16)

    def conv_silu(x_ref, c, w):
        start = pl.multiple_of(c * CHUNK, CHUNK)
        cur = x_ref[pl.ds(start, CHUNK), :]
        prev_start = pl.multiple_of(jnp.maximum(start - CONV_PAD, 0), CONV_PAD)
        prev = x_ref[pl.ds(prev_start, CONV_PAD), :] * jnp.where(c > 0, 1.0, 0.0)
        ext = jnp.concatenate([prev, cur], axis=0)
        y = cur * w[CONV_WIDTH - 1:CONV_WIDTH]
        for d in range(1, CONV_WIDTH):
            y = y + ext[CONV_PAD - d:CONV_PAD - d + CHUNK] * w[CONV_WIDTH - 1 - d:CONV_WIDTH - d]
        return _silu(y)

    def l2n(x):
        return x * lax.rsqrt(jnp.sum(x * x, axis=-1, keepdims=True) + EPS)

    n_prep = math.gcd(n_chunks, GDN_PREP_UNROLL)
    n_inv = math.gcd(n_chunks, GDN_INVERT_UNROLL)

    def prepare(i, _):
        lhs, rhs, kept = [], [], []
        for j in range(n_prep):
            c = i * n_prep + j
            q = l2n(conv_silu(q_ref, c, cw[:, :HEAD_DIM])) * (HEAD_DIM ** -0.5)
            k = l2n(conv_silu(k_ref, c, cw[:, HEAD_DIM:2 * HEAD_DIM]))
            v = conv_silu(v_ref, c, cw[:, 2 * HEAD_DIM:])
            q16 = q.astype(BF16)
            k16 = k.astype(BF16)
            lhs.append(jnp.concatenate([q16, k16], axis=0))
            rhs.append(jnp.concatenate([k16, k16], axis=0))
            kept.append((c, q, k, v))
        qk_kk = _bdot_nt(_stack(lhs), _stack(rhs))
        for j, (c, q, k, v) in enumerate(kept):
            rows = pl.ds(pl.multiple_of(c * CHUNK, CHUNK), CHUNK)
            gcol = gcol_ref[0, 0, rows, :]
            wide = [jnp.broadcast_to(gcol[:, n:n + 1], (CHUNK, HEAD_DIM)) for n in range(4)]
            g_pair = jnp.where(left, wide[0], wide[1])
            beta_pair = jnp.where(left, wide[2], wide[3])
            g_row = grow_ref[0, 0, pl.ds(c, 1), :]
            decay = jnp.exp(jnp.minimum(g_pair - g_row, 0.0))
            at_s[rows, :] = jnp.where(tri, qk_kk[j, :CHUNK] * decay, 0.0).astype(BF16)
            low_s[rows, :] = jnp.where(strict, qk_kk[j, CHUNK:] * decay, 0.0) * beta_pair
            for hh in range(2):
                gc = wide[hh]
                beta = wide[2 + hh]
                e_g = jnp.exp(gc)
                rhs_s[hh, rows, :] = jnp.concatenate([v[:, hh * HEAD_DIM:(hh + 1) * HEAD_DIM] * beta,
                                                      k * (beta * e_g)], axis=1).astype(BF16)
                qe_s[hh, rows, :] = q * e_g
                ko_s[hh, rows, :] = (k * jnp.exp(gc[CHUNK - 1:CHUNK] - gc)).astype(BF16)
        return 0

    def invert(i, _):
        rows = pl.ds(pl.multiple_of(i * (n_inv * CHUNK), n_inv * CHUNK), n_inv * CHUNK)
        low = low_s[rows, :].reshape(n_inv, CHUNK, 2 * CHUNK)
        inv = _tri_inverse_pair(low, eye, left)
        inv_s[rows, :] = inv.reshape(n_inv * CHUNK, 2 * CHUNK).astype(BF16)
        return 0

    def block_diag_wide(x0, x1):
        return jnp.concatenate([jnp.concatenate([x0, zeros_wu], axis=1),
                                jnp.concatenate([zeros_wu, x1], axis=1)], axis=0)

    n_comb = math.gcd(n_chunks, GDN_COMBINE_UNROLL)

    def combine(i, _):
        span = n_comb * CHUNK
        rows_all = pl.ds(pl.multiple_of(i * span, span), span)
        chunk_rows = [pl.ds(pl.multiple_of((i * n_comb + j) * CHUNK, CHUNK), CHUNK) for j in range(n_comb)]
        rhs_bd = _stack([block_diag_wide(rhs_s[0, r, :], rhs_s[1, r, :]) for r in chunk_rows])
        inv = inv_s[rows_all, :].reshape(n_comb, CHUNK, 2 * CHUNK)
        uw = _bdot(inv, rhs_bd).astype(BF16)
        uw_bd = _stack([block_diag_wide(uw[j, :, :2 * HEAD_DIM], uw[j, :, 2 * HEAD_DIM:])
                        for j in range(n_comb)])
        attn = at_s[rows_all, :].reshape(n_comb, CHUNK, 2 * CHUNK)
        a_uw = _bdot(attn, uw_bd)
        k_out = _stack([ko_s[hh, r, :] for r in chunk_rows for hh in range(2)])
        uw_h = _stack([uw[j, :, 2 * hh * HEAD_DIM:2 * (hh + 1) * HEAD_DIM]
                       for j in range(n_comb) for hh in range(2)])
        kt_uw = _bdot_tn(k_out, uw_h)
        for j in range(n_comb):
            c = i * n_comb + j
            base = pl.multiple_of(c * (HEAD_DIM + CHUNK), HEAD_DIM + CHUNK)
            n_rows = pl.ds(pl.multiple_of(c * HEAD_DIM, HEAD_DIM), HEAD_DIM)
            for hh in range(2):
                lo = 2 * hh * HEAD_DIM
                kt = kt_uw[2 * j + hh]
                q_eff = qe_s[hh, chunk_rows[j], :] - a_uw[j, :, lo + HEAD_DIM:lo + 2 * HEAD_DIM]
                lhs_s[hh, pl.ds(base, HEAD_DIM), :] = kt[:, HEAD_DIM:].astype(BF16)
                lhs_s[hh, pl.ds(base + HEAD_DIM, CHUNK), :] = q_eff.astype(BF16)
                n_s[hh, n_rows, :] = kt[:, :HEAD_DIM]
                op_s[hh, chunk_rows[j], :] = a_uw[j, :, lo:lo + HEAD_DIM]
        return 0

    lax.fori_loop(0, n_chunks // n_prep, prepare, 0)
    lax.fori_loop(0, n_chunks // n_inv, invert, 0)
    lax.fori_loop(0, n_chunks // n_comb, combine, 0)

    def recur(c, states):
        rows = pl.ds(pl.multiple_of(c * CHUNK, CHUNK), CHUNK)
        z = z_ref[rows, :]
        base = pl.multiple_of(c * (HEAD_DIM + CHUNK), HEAD_DIM + CHUNK)
        n_rows = pl.ds(pl.multiple_of(c * HEAD_DIM, HEAD_DIM), HEAD_DIM)
        pr = _bdot(lhs_s[:, pl.ds(base, HEAD_DIM + CHUNK), :],
                   _stack([st.astype(BF16) for st in states]))
        outs = []
        new_states = []
        for hh in range(2):
            o = pr[hh, HEAD_DIM:] + op_s[hh, rows, :]
            g_last = gcol_ref[0, 0, pl.ds(c * CHUNK + CHUNK - 1, 1), hh:hh + 1]
            new_states.append(states[hh] * jnp.exp(g_last) + (n_s[hh, n_rows, :] - pr[hh, :HEAD_DIM]))
            outs.append(_rms(o, hn) * _silu(z[:, hh * HEAD_DIM:(hh + 1) * HEAD_DIM]))
        o_ref[rows, :] = jnp.concatenate(outs, axis=1).astype(o_ref.dtype)
        return tuple(new_states)

    zero = jnp.zeros((HEAD_DIM, HEAD_DIM), F32)
    lax.fori_loop(0, n_chunks, recur, (zero, zero), unroll=math.gcd(n_chunks, GDN_RECUR_UNROLL))


def gdn_mixer(proj, conv_w, head_norm, gcol, grow, *, batch, seq, qk_heads):
    nc = seq // CHUNK
    key_w = qk_heads * HEAD_DIM
    cw = jnp.concatenate([
        conv_w[:, :key_w].reshape(CONV_WIDTH, qk_heads, HEAD_DIM),
        conv_w[:, key_w:2 * key_w].reshape(CONV_WIDTH, qk_heads, HEAD_DIM),
        conv_w[:, 2 * key_w:].reshape(CONV_WIDTH, qk_heads, 2 * HEAD_DIM)], axis=2)
    cw = cw.transpose(1, 0, 2)
    v_block0 = 2 * qk_heads // 2
    z_block0 = v_block0 + qk_heads
    assert seq % CHUNK == 0 and gcol.shape == (batch, qk_heads, seq, 4)
    assert grow.shape == (batch, qk_heads, nc, 2 * CHUNK)
    return pl.pallas_call(
        functools.partial(_gdn_body, n_chunks=nc),
        grid=(batch, qk_heads),
        in_specs=[pl.BlockSpec((None, CONV_WIDTH, 4 * HEAD_DIM), lambda b, g: (g, 0, 0)),
                  pl.BlockSpec((1, HEAD_DIM), lambda b, g: (0, 0)),
                  pl.BlockSpec((seq, HEAD_DIM), lambda b, g: (b, g)),
                  pl.BlockSpec((seq, HEAD_DIM), lambda b, g: (b, qk_heads + g)),
                  pl.BlockSpec((seq, 2 * HEAD_DIM), lambda b, g: (b, v_block0 + g)),
                  pl.BlockSpec((seq, 2 * HEAD_DIM), lambda b, g: (b, z_block0 + g)),
                  pl.BlockSpec((1, 1, seq, 4), lambda b, g: (b, g, 0, 0)),
                  pl.BlockSpec((1, 1, nc, 2 * CHUNK), lambda b, g: (b, g, 0, 0))],
        out_specs=pl.BlockSpec((seq, 2 * HEAD_DIM), lambda b, g: (b, g)),
        out_shape=jax.ShapeDtypeStruct((batch * seq, 2 * qk_heads * HEAD_DIM), BF16),
        scratch_shapes=[pltpu.VMEM((seq, 2 * CHUNK), BF16),
                        pltpu.VMEM((seq, 2 * CHUNK), F32),
                        pltpu.VMEM((2, seq, 2 * HEAD_DIM), BF16),
                        pltpu.VMEM((2, seq, HEAD_DIM), F32),
                        pltpu.VMEM((2, seq, HEAD_DIM), BF16),
                        pltpu.VMEM((seq, 2 * CHUNK), BF16),
                        pltpu.VMEM((2, nc * (HEAD_DIM + CHUNK), HEAD_DIM), BF16),
                        pltpu.VMEM((2, nc * HEAD_DIM, HEAD_DIM), F32),
                        pltpu.VMEM((2, seq, HEAD_DIM), F32)],
        compiler_params=_params("parallel", "parallel"),
        name="gdn_delta_rule",
    )(cw, head_norm.reshape(1, HEAD_DIM), proj, proj, proj, proj, gcol, grow)


def _hgrn_layer(h, lb_param, pre, post, w_in, head_norm, w_out, *, batch, seq, layer):
    proj = norm_matmul(h, 0, pre, w_in.astype(BF16), tm=1024, tn=1024, out_dtype=F32,
                       name="hgrn_in_proj_l%d" % layer)
    gated = hgrn_mixer(proj, lb_param, head_norm, batch=batch, seq=seq, layer=layer)
    return out_proj_residual(gated, w_out.astype(BF16), post, h, tm=512,
                             name="hgrn_out_proj_l%d" % layer)


def _pad_rope_cols(w):
    zeros = jnp.zeros(w.shape[:-1] + (ROPE_HALF,), w.dtype)
    return jnp.concatenate([w[..., :ROPE_HALF], zeros, w[..., ROPE_HALF:], zeros], axis=-1)


def _mla_layer(h, positions, pre, post, w_in, q_norm, kv_norm, w_uq, w_ukv, w_out, *, batch, seq):
    d_model = h.shape[1]
    q_rank = q_norm.shape[0]
    kv_rank = kv_norm.shape[0]
    heads = w_out.shape[0] // HEAD_DIM
    lat = q_rank + kv_rank
    gate_w = heads * HEAD_DIM
    w_in_p = jnp.concatenate([w_in[:, :lat], w_in[:, lat + 2 * ROPE_HALF:],
                              _pad_rope_cols(w_in[:, lat:lat + 2 * ROPE_HALF])], axis=1).astype(BF16)
    proj = norm_matmul(h, 0, pre, w_in_p, tm=1024, tn=640, out_dtype=F32, name="mla_in_proj")
    w_uq_h = w_uq.reshape(q_rank, heads, HEAD_DIM + 2 * ROPE_HALF)
    w_uq_p = jnp.concatenate([w_uq_h[..., :HEAD_DIM], _pad_rope_cols(w_uq_h[..., HEAD_DIM:])],
                             axis=-1).reshape(q_rank, heads * 2 * HEAD_DIM).astype(BF16)
    q_up = norm_matmul(proj, 0, q_norm, w_uq_p, tm=1024, tn=1024, out_dtype=F32, name="mla_q_up")
    kv_up = norm_matmul(proj, kv_rank // q_rank, kv_norm, w_ukv.astype(BF16), tm=1024, tn=1024,
                        out_dtype=BF16, name="mla_kv_up")
    pos_col = positions.reshape(batch * seq, 1)
    cos, sin, k_rope = rope_prep(pos_col, proj, (lat + gate_w) // HEAD_DIM, tm=1024)
    gated = mla_attention(q_up, cos, sin, kv_up, k_rope, proj, lat, batch=batch, seq=seq, heads=heads)
    del d_model
    return out_proj_residual(gated, w_out.astype(BF16), post, h, tm=512, name="mla_out_proj")


def _gdn_layer(h, pre, post, w_in, conv_w, a_log, dt_bias, head_norm, w_out, *, batch, seq):
    v_heads = a_log.shape[0]
    qk_heads = v_heads // 2
    main = 2 * qk_heads * HEAD_DIM + 2 * v_heads * HEAD_DIM
    proj = norm_matmul(h, 0, pre, w_in[:, :main].astype(BF16), tm=1024, tn=1024, out_dtype=F32,
                       name="gdn_in_proj")
    w_ab = jnp.pad(w_in[:, main:], ((0, 0), (0, HEAD_DIM - 2 * v_heads))).astype(BF16)
    ab = norm_matmul(h, 0, pre, w_ab, tm=1024, tn=HEAD_DIM, out_dtype=F32,
                     name="gdn_gate_proj")[:, :2 * v_heads]
    nc = seq // CHUNK
    col = gdn_gates(ab, a_log, dt_bias, batch=batch, seq=seq)
    gcol = col.reshape(batch, seq, 2, qk_heads, 2).transpose(0, 3, 1, 2, 4)
    gcol = gcol.reshape(batch, qk_heads, seq, 4)
    grow = col[:, :v_heads].reshape(batch, nc, CHUNK, qk_heads, 2).transpose(0, 3, 1, 4, 2)
    grow = grow.reshape(batch, qk_heads, nc, 2 * CHUNK)
    gated = gdn_mixer(proj, conv_w, head_norm, gcol, grow, batch=batch, seq=seq, qk_heads=qk_heads)
    return out_proj_residual(gated, w_out.astype(BF16), post, h, tm=512, name="gdn_out_proj")


def kernel(x, positions, hgrn_lb, l0_pre_norm, l0_post_norm, l0_w_in, l0_head_norm, l0_w_out, l1_pre_norm, l1_post_norm, l1_w_in, l1_q_norm, l1_kv_norm, l1_w_uq, l1_w_ukv, l1_w_out, l2_pre_norm, l2_post_norm, l2_w_in, l2_conv_w, l2_a_log, l2_dt_bias, l2_head_norm, l2_w_out, l3_pre_norm, l3_post_norm, l3_w_in, l3_head_norm, l3_w_out):
    batch, seq, d_model = x.shape
    h = x.reshape(batch * seq, d_model)
    h = _hgrn_layer(h, hgrn_lb, l0_pre_norm, l0_post_norm, l0_w_in, l0_head_norm, l0_w_out,
                    batch=batch, seq=seq, layer=0)
    h = _mla_layer(h, positions, l1_pre_norm, l1_post_norm, l1_w_in, l1_q_norm, l1_kv_norm,
                   l1_w_uq, l1_w_ukv, l1_w_out, batch=batch, seq=seq)
    h = _gdn_layer(h, l2_pre_norm, l2_post_norm, l2_w_in, l2_conv_w, l2_a_log, l2_dt_bias,
                   l2_head_norm, l2_w_out, batch=batch, seq=seq)
    h = _hgrn_layer(h, hgrn_lb, l3_pre_norm, l3_post_norm, l3_w_in, l3_head_norm, l3_w_out,
                    batch=batch, seq=seq, layer=3)
    return h.reshape(batch, seq, d_model)
```

```python
import functools
import math

import jax
import jax.numpy as jnp
from jax import lax
from jax.experimental import pallas as pl
from jax.experimental.pallas import tpu as pltpu

F32 = jnp.float32
BF16 = jnp.bfloat16

EPS = 1e-6
CHUNK = 64
SUB = 16
N_SUB = CHUNK // SUB
HEAD_DIM = 128
DEPTH = 4
ROPE_HALF = 32
ROPE_THETA = 10000.0
MLA_SCALE = (128 + 64) ** -0.5
CONV_WIDTH = 4
CONV_PAD = 8
MAX_EXP_ARG = 80.0
VMEM_LIMIT_BYTES = 56 * 1024 * 1024
HGRN_UNROLL = 16
GDN_PREP_UNROLL = 4
GDN_INVERT_UNROLL = 16
GDN_COMBINE_UNROLL = 8
GDN_RECUR_UNROLL = 8
MLA_TQ = 256
MLA_TK = 512
MLA_HEADS_PER_STEP = 4

NT_DIMS = (((1,), (1,)), ((), ()))
TN_DIMS = (((0,), (0,)), ((), ()))


def _params(*semantics):
    return pltpu.CompilerParams(dimension_semantics=semantics,
                                vmem_limit_bytes=VMEM_LIMIT_BYTES)


def _sigmoid(x):
    return 1.0 / (1.0 + jnp.exp(-x))


def _silu(x):
    return x * _sigmoid(x)


def _rms(x, gain):
    ms = jnp.mean(x * x, axis=-1, keepdims=True)
    return x * lax.rsqrt(ms + EPS) * gain


def _dot(a, b):
    return jnp.dot(a, b, preferred_element_type=F32)


def _dot_nt(a, b):
    return lax.dot_general(a, b, NT_DIMS, preferred_element_type=F32)


def _dot_tn(a, b):
    return lax.dot_general(a, b, TN_DIMS, preferred_element_type=F32)


def _stack(xs):
    m, n = xs[0].shape
    return jnp.concatenate(xs, axis=0).reshape(len(xs), m, n)


def _bdot(a, b):
    return lax.dot_general(a, b, (((2,), (1,)), ((0,), (0,))), preferred_element_type=F32)


def _bdot_nt(a, b):
    return lax.dot_general(a, b, (((2,), (2,)), ((0,), (0,))), preferred_element_type=F32)


def _bdot_tn(a, b):
    return lax.dot_general(a, b, (((1,), (1,)), ((0,), (0,))), preferred_element_type=F32)


def _chunk_cumsum(x, row):
    shift = 1
    while shift < CHUNK:
        x = x + jnp.where(row >= shift, pltpu.roll(x, shift, axis=0), 0.0)
        shift *= 2
    return x


def _norm_matmul_body(x_ref, g_ref, w_ref, o_ref, xn_ref):
    @pl.when(pl.program_id(1) == 0)
    def _():
        xn_ref[...] = _rms(x_ref[...], g_ref[...]).astype(BF16)

    o_ref[...] = _dot(xn_ref[...], w_ref[...]).astype(o_ref.dtype)


def norm_matmul(x, col_block, gain, w, *, tm, tn, out_dtype, name):
    t = x.shape[0]
    k, n = w.shape
    tm, tn = min(tm, t), min(tn, n)
    return pl.pallas_call(
        _norm_matmul_body,
        grid=(t // tm, n // tn),
        in_specs=[pl.BlockSpec((tm, k), lambda i, j: (i, col_block)),
                  pl.BlockSpec((1, k), lambda i, j: (0, 0)),
                  pl.BlockSpec((k, tn), lambda i, j: (0, j))],
        out_specs=pl.BlockSpec((tm, tn), lambda i, j: (i, j)),
        out_shape=jax.ShapeDtypeStruct((t, n), out_dtype),
        scratch_shapes=[pltpu.VMEM((tm, k), BF16)],
        compiler_params=_params("parallel", "arbitrary"),
        name=name,
    )(x, gain.reshape(1, k), w)


def _out_proj_body(g_ref, w_ref, p_ref, h_ref, o_ref):
    y = _dot(g_ref[...], w_ref[...])
    o_ref[...] = h_ref[...] + _rms(y, p_ref[...])


def out_proj_residual(g, w, post, h, *, tm, name):
    t, k = g.shape
    d = w.shape[1]
    tm = min(tm, t)
    return pl.pallas_call(
        _out_proj_body,
        grid=(t // tm,),
        in_specs=[pl.BlockSpec((tm, k), lambda i: (i, 0)),
                  pl.BlockSpec((k, d), lambda i: (0, 0), pipeline_mode=pl.Buffered(1)),
                  pl.BlockSpec((1, d), lambda i: (0, 0)),
                  pl.BlockSpec((tm, d), lambda i: (i, 0))],
        out_specs=pl.BlockSpec((tm, d), lambda i: (i, 0)),
        out_shape=jax.ShapeDtypeStruct((t, d), F32),
        compiler_params=_params("parallel"),
        name=name,
    )(g, w, post.reshape(1, d), h)


def _hgrn_body(lbp_ref, hn_ref, q_ref, f_ref, i_ref, z_ref, o_ref, *, layer, n_chunks):
    p = lbp_ref[...]
    e = jnp.exp(p - jnp.max(p, axis=0, keepdims=True))
    sm = e / jnp.sum(e, axis=0, keepdims=True)
    lb = jnp.zeros((1, HEAD_DIM), F32)
    for j in range(1, layer + 1):
        lb = lb + sm[j:j + 1]
    log_lb = jnp.log(lb)
    log_1m_lb = jnp.log(1.0 - lb)
    one_m_lb = 1.0 - lb
    hn = hn_ref[...]

    n_par = math.gcd(n_chunks, HGRN_UNROLL)
    span = n_par * CHUNK
    row = lax.broadcasted_iota(jnp.int32, (span, HEAD_DIM), 0) % CHUNK
    t_idx = lax.broadcasted_iota(jnp.int32, (CHUNK, N_SUB * CHUNK), 0)
    c_idx = lax.broadcasted_iota(jnp.int32, (CHUNK, N_SUB * CHUNK), 1)
    valid = ((c_idx // CHUNK) == (t_idx // SUB)) & ((c_idx % CHUNK) <= t_idx)
    scale = HEAD_DIM ** -0.5

    def step(i, state_t):
        rows = pl.ds(pl.multiple_of(i * span, span), span)
        q = q_ref[rows, :]
        f = f_ref[rows, :]
        v16 = i_ref[rows, :].astype(BF16)

        ef = jnp.exp(-jnp.abs(f))
        inv = 1.0 / (1.0 + ef)
        log_sig = jnp.minimum(f, 0.0) - jnp.log(1.0 + ef)
        cc = log_1m_lb + log_sig
        log_f = jnp.maximum(log_lb, cc) + jnp.log(1.0 + jnp.exp(-jnp.abs(log_lb - cc)))
        k = one_m_lb * jnp.where(f >= 0.0, ef * inv, inv)
        qs = _silu(q) * scale
        b = _chunk_cumsum(log_f, row)

        q_in, k_stack, v_stack, q_out, k_out, decay_last = [], [], [], [], [], []
        for j in range(n_par):
            lo = j * CHUNK
            bj = b[lo:lo + CHUNK]
            kj = k[lo:lo + CHUNK]
            qj = qs[lo:lo + CHUNK]
            refs = [jnp.zeros((1, HEAD_DIM), F32)] + [bj[SUB * s - 1:SUB * s] for s in range(1, N_SUB)]
            ref_rows = jnp.concatenate([jnp.broadcast_to(r, (SUB, HEAD_DIM)) for r in refs], axis=0)
            q_in.append((qj * jnp.exp(bj - ref_rows)).astype(BF16))
            k_stack.append(jnp.concatenate(
                [kj * jnp.exp(jnp.minimum(r - bj, MAX_EXP_ARG)) for r in refs], axis=0).astype(BF16))
            v_stack.append(jnp.concatenate([v16[lo:lo + CHUNK]] * N_SUB, axis=0))
            q_out.append((qj * jnp.exp(bj)).astype(BF16))
            b_last = bj[CHUNK - 1:CHUNK]
            k_out.append((kj * jnp.exp(b_last - bj)).astype(BF16))
            decay_last.append(jnp.exp(b_last))

        scores = _bdot_nt(_stack(q_in), _stack(k_stack))
        scores = jnp.where(valid, scores, 0.0).astype(BF16)
        o = _bdot(scores, _stack(v_stack))
        d_state = _bdot_tn(v16.reshape(n_par, CHUNK, HEAD_DIM), _stack(k_out))
        states = [state_t]
        for j in range(n_par):
            states.append(states[j] * decay_last[j] + d_state[j])
        o = o + _bdot_nt(_stack(q_out), _stack([s.astype(BF16) for s in states[:n_par]]))

        o = _rms(o.reshape(span, HEAD_DIM), hn) * _silu(z_ref[rows, :])
        o_ref[rows, :] = o.astype(o_ref.dtype)
        return states[n_par]

    lax.fori_loop(0, n_chunks // n_par, step, jnp.zeros((HEAD_DIM, HEAD_DIM), F32))


def hgrn_mixer(proj, lb_param, head_norm, *, batch, seq, layer):
    heads = lb_param.shape[1] // HEAD_DIM
    col = lambda off: pl.BlockSpec((seq, HEAD_DIM), lambda b, h: (b, off * heads + h))
    return pl.pallas_call(
        functools.partial(_hgrn_body, layer=layer, n_chunks=seq // CHUNK),
        grid=(batch, heads),
        in_specs=[pl.BlockSpec((DEPTH, HEAD_DIM), lambda b, h: (0, h)),
                  pl.BlockSpec((1, HEAD_DIM), lambda b, h: (0, 0)),
                  col(0), col(1), col(2), col(3)],
        out_specs=pl.BlockSpec((seq, HEAD_DIM), lambda b, h: (b, h)),
        out_shape=jax.ShapeDtypeStruct((batch * seq, heads * HEAD_DIM), BF16),
        compiler_params=_params("parallel", "parallel"),
        name="hgrn_gla_l%d" % layer,
    )(lb_param, head_norm.reshape(1, HEAD_DIM), proj, proj, proj, proj)


def _rope_tables(pos_col):
    lane = lax.broadcasted_iota(jnp.int32, (1, HEAD_DIM), 1)
    j = (lane % (2 * ROPE_HALF)).astype(F32)
    inv_freq = jnp.exp(j * (-2.0 * math.log(ROPE_THETA) / (2 * ROPE_HALF)))
    live = (lane % (2 * ROPE_HALF)) < ROPE_HALF
    ang = pos_col * inv_freq
    cos = jnp.where(live, jnp.cos(ang), 0.0)
    sin = jnp.where(live, jnp.sin(ang), 0.0)
    sin = jnp.where(lane < 2 * ROPE_HALF, -sin, sin)
    return cos, sin


def _apply_rope(t, cos, sin):
    return t * cos + pltpu.roll(t, 2 * ROPE_HALF, axis=1) * sin


def _rope_prep_body(pos_ref, kr_ref, cos_ref, sin_ref, k_ref):
    cos, sin = _rope_tables(pos_ref[...].astype(F32))
    cos_ref[...] = cos
    sin_ref[...] = sin
    k_ref[...] = _apply_rope(kr_ref[...], cos, sin).astype(BF16)


def rope_prep(pos_col, proj, kr_block, *, tm):
    t = pos_col.shape[0]
    tm = min(tm, t)
    tab = pl.BlockSpec((tm, HEAD_DIM), lambda i: (i, 0))
    return pl.pallas_call(
        _rope_prep_body,
        grid=(t // tm,),
        in_specs=[pl.BlockSpec((tm, 1), lambda i: (i, 0)),
                  pl.BlockSpec((tm, HEAD_DIM), lambda i: (i, kr_block))],
        out_specs=[tab, tab, tab],
        out_shape=[jax.ShapeDtypeStruct((t, HEAD_DIM), F32),
                   jax.ShapeDtypeStruct((t, HEAD_DIM), F32),
                   jax.ShapeDtypeStruct((t, HEAD_DIM), BF16)],
        compiler_params=_params("parallel"),
        name="mla_rope_prep",
    )(pos_col, proj)


def _mla_attn_body(q_ref, cos_ref, sin_ref, kv_ref, kr_ref, z_ref, o_ref, *, tq, tk, n_heads):
    qi = pl.program_id(2)
    cos = cos_ref[...]
    sin = sin_ref[...]
    q_all = q_ref[...]
    q_cat = []
    for a in range(n_heads):
        q = q_all[:, 2 * a * HEAD_DIM:2 * (a + 1) * HEAD_DIM]
        q_rope = _apply_rope(q[:, HEAD_DIM:], cos, sin)
        q_cat.append((jnp.concatenate([q[:, :HEAD_DIM], q_rope], axis=1) * MLA_SCALE).astype(BF16))
    q_cat = _stack(q_cat)
    ones = jnp.ones((tk, HEAD_DIM), BF16)
    q_chunk = (qi * tq + lax.broadcasted_iota(jnp.int32, (tq, tk), 0)) // CHUNK
    k_in_block = lax.broadcasted_iota(jnp.int32, (tq, tk), 1)

    def block(j, carry, masked):
        m, acc = carry
        rows = pl.ds(pl.multiple_of(j * tk, tk), tk)
        kr = kr_ref[rows, :]
        k_cat, v_ext = [], []
        for a in range(n_heads):
            lo = 2 * a * HEAD_DIM
            k_cat.append(jnp.concatenate([kv_ref[rows, lo:lo + HEAD_DIM], kr], axis=1))
            v_ext.append(jnp.concatenate([kv_ref[rows, lo + HEAD_DIM:lo + 2 * HEAD_DIM], ones], axis=1))
        s = _bdot_nt(q_cat, _stack(k_cat))
        if masked:
            s = jnp.where((j * tk + k_in_block) // CHUNK <= q_chunk, s, -jnp.inf)
        m_new = jnp.maximum(m, jnp.max(s, axis=-1, keepdims=True))
        p = jnp.exp(s - m_new).astype(BF16)
        acc = jnp.exp(m - m_new) * acc + _bdot(p, _stack(v_ext))
        return m_new, acc

    init = (jnp.full((n_heads, tq, 1), -jnp.inf, F32), jnp.zeros((n_heads, tq, 2 * HEAD_DIM), F32))
    n_full = (qi * tq) // tk
    carry = lax.fori_loop(0, n_full, lambda j, c: block(j, c, False), init)
    _, acc = block(n_full, carry, True)
    z = z_ref[...]
    outs = [acc[a, :, :HEAD_DIM] / acc[a, :, HEAD_DIM:] * _silu(z[:, a * HEAD_DIM:(a + 1) * HEAD_DIM])
            for a in range(n_heads)]
    o_ref[...] = jnp.concatenate(outs, axis=1).astype(o_ref.dtype)


def mla_attention(q_up, cos, sin, kv_up, k_rope, proj, z_col0, *, batch, seq, heads):
    tq, tk, hps = min(MLA_TQ, seq), min(MLA_TK, seq), MLA_HEADS_PER_STEP
    assert seq % tk == 0 and tk % tq == 0 and tq % CHUNK == 0 and heads % hps == 0
    assert z_col0 % (hps * HEAD_DIM) == 0
    nq = seq // tq
    z_block0 = z_col0 // (hps * HEAD_DIM)
    return pl.pallas_call(
        functools.partial(_mla_attn_body, tq=tq, tk=tk, n_heads=hps),
        grid=(batch, heads // hps, nq),
        in_specs=[pl.BlockSpec((tq, hps * 2 * HEAD_DIM), lambda b, h, i: (b * nq + i, h)),
                  pl.BlockSpec((tq, HEAD_DIM), lambda b, h, i: (b * nq + i, 0)),
                  pl.BlockSpec((tq, HEAD_DIM), lambda b, h, i: (b * nq + i, 0)),
                  pl.BlockSpec((seq, hps * 2 * HEAD_DIM), lambda b, h, i: (b, h)),
                  pl.BlockSpec((seq, HEAD_DIM), lambda b, h, i: (b, 0)),
                  pl.BlockSpec((tq, hps * HEAD_DIM), lambda b, h, i: (b * nq + i, z_block0 + h))],
        out_specs=pl.BlockSpec((tq, hps * HEAD_DIM), lambda b, h, i: (b * nq + i, h)),
        out_shape=jax.ShapeDtypeStruct((batch * seq, heads * HEAD_DIM), BF16),
        compiler_params=_params("parallel", "parallel", "arbitrary"),
        name="mla_attention",
    )(q_up, cos, sin, kv_up, k_rope, proj)


def _gdn_gates_body(ab_ref, alog_ref, dt_ref, o_ref, *, n_heads):
    ab = ab_ref[...]
    x = ab[:, :n_heads] + dt_ref[...]
    softplus = jnp.maximum(x, 0.0) + jnp.log(1.0 + jnp.exp(-jnp.abs(x)))
    g = -jnp.exp(alog_ref[...]) * softplus
    row = lax.broadcasted_iota(jnp.int32, g.shape, 0) % CHUNK
    o_ref[...] = jnp.concatenate([_chunk_cumsum(g, row), _sigmoid(ab[:, n_heads:2 * n_heads])], axis=1)


def gdn_gates(ab, a_log, dt_bias, *, batch, seq):
    n_heads = a_log.shape[0]
    vec = pl.BlockSpec((1, n_heads), lambda b: (0, 0))
    return pl.pallas_call(
        functools.partial(_gdn_gates_body, n_heads=n_heads),
        grid=(batch,),
        in_specs=[pl.BlockSpec((seq, ab.shape[1]), lambda b: (b, 0)), vec, vec],
        out_specs=pl.BlockSpec((seq, 2 * n_heads), lambda b: (b, 0)),
        out_shape=jax.ShapeDtypeStruct((batch * seq, 2 * n_heads), F32),
        compiler_params=_params("parallel"),
        name="gdn_gates",
    )(ab, a_log.reshape(1, n_heads), dt_bias.reshape(1, n_heads))


def _block_diag2(x, left):
    return jnp.concatenate([jnp.where(left, x, 0.0), jnp.where(left, 0.0, x)], axis=1)


def _tri_inverse_pair(low, eye, left):
    m = -low
    s = eye + m
    p = _bdot(m.astype(BF16), _block_diag2(m, left).astype(BF16))
    n = 2
    while 2 * n < CHUNK:
        rhs = jnp.concatenate([_block_diag2(p, left), _block_diag2(s, left)], axis=2).astype(BF16)
        out = _bdot(p.astype(BF16), rhs)
        p = out[:, :, :2 * CHUNK]
        s = s + out[:, :, 2 * CHUNK:]
        n *= 2
    return s + _bdot(p.astype(BF16), _block_diag2(s, left).astype(BF16))


def _gdn_body(cw_ref, hn_ref, q_ref, k_ref, v_ref, z_ref, gcol_ref, grow_ref, o_ref,
              at_s, low_s, rhs_s, qe_s, ko_s, inv_s, lhs_s, n_s, op_s, *, n_chunks):
    cw = cw_ref[...]
    hn = hn_ref[...]
    row = lax.broadcasted_iota(jnp.int32, (CHUNK, 2 * CHUNK), 0)
    lane = lax.broadcasted_iota(jnp.int32, (CHUNK, 2 * CHUNK), 1)
    left = lane < CHUNK
    col = lane % CHUNK
    tri = col <= row
    strict = col < row
    eye = (col == row).astype(F32)
    zeros_wu = jnp.zeros((CHUNK, 2 * HEAD_DIM), BF16)

    def conv_silu(x_ref, c, w):
        start = pl.multiple_of(c * CHUNK, CHUNK)
        cur = x_ref[pl.ds(start, CHUNK), :]
        prev_start = pl.multiple_of(jnp.maximum(start - CONV_PAD, 0), CONV_PAD)
        prev = x_ref[pl.ds(prev_start, CONV_PAD), :] * jnp.where(c > 0, 1.0, 0.0)
        ext = jnp.concatenate([prev, cur], axis=0)
        y = cur * w[CONV_WIDTH - 1:CONV_WIDTH]
        for d in range(1, CONV_WIDTH):
            y = y + ext[CONV_PAD - d:CONV_PAD - d + CHUNK] * w[CONV_WIDTH - 1 - d:CONV_WIDTH - d]
        return _silu(y)

    def l2n(x):
        return x * lax.rsqrt(jnp.sum(x * x, axis=-1, keepdims=True) + EPS)

    n_prep = math.gcd(n_chunks, GDN_PREP_UNROLL)
    n_inv = math.gcd(n_chunks, GDN_INVERT_UNROLL)

    def prepare(i, _):
        lhs, rhs, kept = [], [], []
        for j in range(n_prep):
            c = i * n_prep + j
            q = l2n(conv_silu(q_ref, c, cw[:, :HEAD_DIM])) * (HEAD_DIM ** -0.5)
            k = l2n(conv_silu(k_ref, c, cw[:, HEAD_DIM:2 * HEAD_DIM]))
            v = conv_silu(v_ref, c, cw[:, 2 * HEAD_DIM:])
            q16 = q.astype(BF16)
            k16 = k.astype(BF16)
            lhs.append(jnp.concatenate([q16, k16], axis=0))
            rhs.append(jnp.concatenate([k16, k16], axis=0))
            kept.append((c, q, k, v))
        qk_kk = _bdot_nt(_stack(lhs), _stack(rhs))
        for j, (c, q, k, v) in enumerate(kept):
            rows = pl.ds(pl.multiple_of(c * CHUNK, CHUNK), CHUNK)
            gcol = gcol_ref[0, 0, rows, :]
            wide = [jnp.broadcast_to(gcol[:, n:n + 1], (CHUNK, HEAD_DIM)) for n in range(4)]
            g_pair = jnp.where(left, wide[0], wide[1])
            beta_pair = jnp.where(left, wide[2], wide[3])
            g_row = grow_ref[0, 0, pl.ds(c, 1), :]
            decay = jnp.exp(jnp.minimum(g_pair - g_row, 0.0))
            at_s[rows, :] = jnp.where(tri, qk_kk[j, :CHUNK] * decay, 0.0).astype(BF16)
            low_s[rows, :] = jnp.where(strict, qk_kk[j, CHUNK:] * decay, 0.0) * beta_pair
            for hh in range(2):
                gc = wide[hh]
                beta = wide[2 + hh]
                e_g = jnp.exp(gc)
                rhs_s[hh, rows, :] = jnp.concatenate([v[:, hh * HEAD_DIM:(hh + 1) * HEAD_DIM] * beta,
                                                      k * (beta * e_g)], axis=1).astype(BF16)
                qe_s[hh, rows, :] = q * e_g
                ko_s[hh, rows, :] = (k * jnp.exp(gc[CHUNK - 1:CHUNK] - gc)).astype(BF16)
        return 0

    def invert(i, _):
        rows = pl.ds(pl.multiple_of(i * (n_inv * CHUNK), n_inv * CHUNK), n_inv * CHUNK)
        low = low_s[rows, :].reshape(n_inv, CHUNK, 2 * CHUNK)
        inv = _tri_inverse_pair(low, eye, left)
        inv_s[rows, :] = inv.reshape(n_inv * CHUNK, 2 * CHUNK).astype(BF16)
        return 0

    def block_diag_wide(x0, x1):
        return jnp.concatenate([jnp.concatenate([x0, zeros_wu], axis=1),
                                jnp.concatenate([zeros_wu, x1], axis=1)], axis=0)

    n_comb = math.gcd(n_chunks, GDN_COMBINE_UNROLL)

    def combine(i, _):
        span = n_comb * CHUNK
        rows_all = pl.ds(pl.multiple_of(i * span, span), span)
        chunk_rows = [pl.ds(pl.multiple_of((i * n_comb + j) * CHUNK, CHUNK), CHUNK) for j in range(n_comb)]
        rhs_bd = _stack([block_diag_wide(rhs_s[0, r, :], rhs_s[1, r, :]) for r in chunk_rows])
        inv = inv_s[rows_all, :].reshape(n_comb, CHUNK, 2 * CHUNK)
        uw = _bdot(inv, rhs_bd).astype(BF16)
        uw_bd = _stack([block_diag_wide(uw[j, :, :2 * HEAD_DIM], uw[j, :, 2 * HEAD_DIM:])
                        for j in range(n_comb)])
        attn = at_s[rows_all, :].reshape(n_comb, CHUNK, 2 * CHUNK)
        a_uw = _bdot(attn, uw_bd)
        k_out = _stack([ko_s[hh, r, :] for r in chunk_rows for hh in range(2)])
        uw_h = _stack([uw[j, :, 2 * hh * HEAD_DIM:2 * (hh + 1) * HEAD_DIM]
                       for j in range(n_comb) for hh in range(2)])
        kt_uw = _bdot_tn(k_out, uw_h)
        for j in range(n_comb):
            c = i * n_comb + j
            base = pl.multiple_of(c * (HEAD_DIM + CHUNK), HEAD_DIM + CHUNK)
            n_rows = pl.ds(pl.multiple_of(c * HEAD_DIM, HEAD_DIM), HEAD_DIM)
            for hh in range(2):
                lo = 2 * hh * HEAD_DIM
                kt = kt_uw[2 * j + hh]
                q_eff = qe_s[hh, chunk_rows[j], :] - a_uw[j, :, lo + HEAD_DIM:lo + 2 * HEAD_DIM]
                lhs_s[hh, pl.ds(base, HEAD_DIM), :] = kt[:, HEAD_DIM:].astype(BF16)
                lhs_s[hh, pl.ds(base + HEAD_DIM, CHUNK), :] = q_eff.astype(BF16)
                n_s[hh, n_rows, :] = kt[:, :HEAD_DIM]
                op_s[hh, chunk_rows[j], :] = a_uw[j, :, lo:lo + HEAD_DIM]
        return 0

    lax.fori_loop(0, n_chunks // n_prep, prepare, 0)
    lax.fori_loop(0, n_chunks // n_inv, invert, 0)
    lax.fori_loop(0, n_chunks // n_comb, combine, 0)

    def recur(c, states):
        rows = pl.ds(pl.multiple_of(c * CHUNK, CHUNK), CHUNK)
        z = z_ref[rows, :]
        base = pl.multiple_of(c * (HEAD_DIM + CHUNK), HEAD_DIM + CHUNK)
        n_rows = pl.ds(pl.multiple_of(c * HEAD_DIM, HEAD_DIM), HEAD_DIM)
        pr = _bdot(lhs_s[:, pl.ds(base, HEAD_DIM + CHUNK), :],
                   _stack([st.astype(BF16) for st in states]))
        outs = []
        new_states = []
        for hh in range(2):
            o = pr[hh, HEAD_DIM:] + op_s[hh, rows, :]
            g_last = gcol_ref[0, 0, pl.ds(c * CHUNK + CHUNK - 1, 1), hh:hh + 1]
            new_states.append(states[hh] * jnp.exp(g_last) + (n_s[hh, n_rows, :] - pr[hh, :HEAD_DIM]))
            outs.append(_rms(o, hn) * _silu(z[:, hh * HEAD_DIM:(hh + 1) * HEAD_DIM]))
        o_ref[rows, :] = jnp.concatenate(outs, axis=1).astype(o_ref.dtype)
        return tuple(new_states)

    zero = jnp.zeros((HEAD_DIM, HEAD_DIM), F32)
    lax.fori_loop(0, n_chunks, recur, (zero, zero), unroll=math.gcd(n_chunks, GDN_RECUR_UNROLL))


def gdn_mixer(proj, conv_w, head_norm, gcol, grow, *, batch, seq, qk_heads):
    nc = seq // CHUNK
    key_w = qk_heads * HEAD_DIM
    cw = jnp.concatenate([
        conv_w[:, :key_w].reshape(CONV_WIDTH, qk_heads, HEAD_DIM),
        conv_w[:, key_w:2 * key_w].reshape(CONV_WIDTH, qk_heads, HEAD_DIM),
        conv_w[:, 2 * key_w:].reshape(CONV_WIDTH, qk_heads, 2 * HEAD_DIM)], axis=2)
    cw = cw.transpose(1, 0, 2)
    v_block0 = 2 * qk_heads // 2
    z_block0 = v_block0 + qk_heads
    assert seq % CHUNK == 0 and gcol.shape == (batch, qk_heads, seq, 4)
    assert grow.shape == (batch, qk_heads, nc, 2 * CHUNK)
    return pl.pallas_call(
        functools.partial(_gdn_body, n_chunks=nc),
        grid=(batch, qk_heads),
        in_specs=[pl.BlockSpec((None, CONV_WIDTH, 4 * HEAD_DIM), lambda b, g: (g, 0, 0)),
                  pl.BlockSpec((1, HEAD_DIM), lambda b, g: (0, 0)),
                  pl.BlockSpec((seq, HEAD_DIM), lambda b, g: (b, g)),
                  pl.BlockSpec((seq, HEAD_DIM), lambda b, g: (b, qk_heads + g)),
                  pl.BlockSpec((seq, 2 * HEAD_DIM), lambda b, g: (b, v_block0 + g)),
                  pl.BlockSpec((seq, 2 * HEAD_DIM), lambda b, g: (b, z_block0 + g)),
                  pl.BlockSpec((1, 1, seq, 4), lambda b, g: (b, g, 0, 0)),
                  pl.BlockSpec((1, 1, nc, 2 * CHUNK), lambda b, g: (b, g, 0, 0))],
        out_specs=pl.BlockSpec((seq, 2 * HEAD_DIM), lambda b, g: (b, g)),
        out_shape=jax.ShapeDtypeStruct((batch * seq, 2 * qk_heads * HEAD_DIM), BF16),
        scratch_shapes=[pltpu.VMEM((seq, 2 * CHUNK), BF16),
                        pltpu.VMEM((seq, 2 * CHUNK), F32),
                        pltpu.VMEM((2, seq, 2 * HEAD_DIM), BF16),
                        pltpu.VMEM((2, seq, HEAD_DIM), F32),
                        pltpu.VMEM((2, seq, HEAD_DIM), BF16),
                        pltpu.VMEM((seq, 2 * CHUNK), BF16),
                        pltpu.VMEM((2, nc * (HEAD_DIM + CHUNK), HEAD_DIM), BF16),
                        pltpu.VMEM((2, nc * HEAD_DIM, HEAD_DIM), F32),
                        pltpu.VMEM((2, seq, HEAD_DIM), F32)],
        compiler_params=_params("parallel", "parallel"),
        name="gdn_delta_rule",
    )(cw, head_norm.reshape(1, HEAD_DIM), proj, proj, proj, proj, gcol, grow)


def _hgrn_layer(h, lb_param, pre, post, w_in, head_norm, w_out, *, batch, seq, layer):
    proj = norm_matmul(h, 0, pre, w_in.astype(BF16), tm=1024, tn=1024, out_dtype=F32,
                       name="hgrn_in_proj_l%d" % layer)
    gated = hgrn_mixer(proj, lb_param, head_norm, batch=batch, seq=seq, layer=layer)
    return out_proj_residual(gated, w_out.astype(BF16), post, h, tm=512,
                             name="hgrn_out_proj_l%d" % layer)


def _pad_rope_cols(w):
    zeros = jnp.zeros(w.shape[:-1] + (ROPE_HALF,), w.dtype)
    return jnp.concatenate([w[..., :ROPE_HALF], zeros, w[..., ROPE_HALF:], zeros], axis=-1)


def _mla_layer(h, positions, pre, post, w_in, q_norm, kv_norm, w_uq, w_ukv, w_out, *, batch, seq):
    d_model = h.shape[1]
    q_rank = q_norm.shape[0]
    kv_rank = kv_norm.shape[0]
    heads = w_out.shape[0] // HEAD_DIM
    lat = q_rank + kv_rank
    gate_w = heads * HEAD_DIM
    w_in_p = jnp.concatenate([w_in[:, :lat], w_in[:, lat + 2 * ROPE_HALF:],
                              _pad_rope_cols(w_in[:, lat:lat + 2 * ROPE_HALF])], axis=1).astype(BF16)
    proj = norm_matmul(h, 0, pre, w_in_p, tm=1024, tn=640, out_dtype=F32, name="mla_in_proj")
    w_uq_h = w_uq.reshape(q_rank, heads, HEAD_DIM + 2 * ROPE_HALF)
    w_uq_p = jnp.concatenate([w_uq_h[..., :HEAD_DIM], _pad_rope_cols(w_uq_h[..., HEAD_DIM:])],
                             axis=-1).reshape(q_rank, heads * 2 * HEAD_DIM).astype(BF16)
    q_up = norm_matmul(proj, 0, q_norm, w_uq_p, tm=1024, tn=1024, out_dtype=F32, name="mla_q_up")
    kv_up = norm_matmul(proj, kv_rank // q_rank, kv_norm, w_ukv.astype(BF16), tm=1024, tn=1024,
                        out_dtype=BF16, name="mla_kv_up")
    pos_col = positions.reshape(batch * seq, 1)
    cos, sin, k_rope = rope_prep(pos_col, proj, (lat + gate_w) // HEAD_DIM, tm=1024)
    gated = mla_attention(q_up, cos, sin, kv_up, k_rope, proj, lat, batch=batch, seq=seq, heads=heads)
    del d_model
    return out_proj_residual(gated, w_out.astype(BF16), post, h, tm=512, name="mla_out_proj")


def _gdn_layer(h, pre, post, w_in, conv_w, a_log, dt_bias, head_norm, w_out, *, batch, seq):
    v_heads = a_log.shape[0]
    qk_heads = v_heads // 2
    main = 2 * qk_heads * HEAD_DIM + 2 * v_heads * HEAD_DIM
    proj = norm_matmul(h, 0, pre, w_in[:, :main].astype(BF16), tm=1024, tn=1024, out_dtype=F32,
                       name="gdn_in_proj")
    w_ab = jnp.pad(w_in[:, main:], ((0, 0), (0, HEAD_DIM - 2 * v_heads))).astype(BF16)
    ab = norm_matmul(h, 0, pre, w_ab, tm=1024, tn=HEAD_DIM, out_dtype=F32,
                     name="gdn_gate_proj")
    nc = seq // CHUNK
    col = gdn_gates(ab, a_log, dt_bias, batch=batch, seq=seq)
    gcol = col.reshape(batch, seq, 2, qk_heads, 2).transpose(0, 3, 1, 2, 4)
    gcol = gcol.reshape(batch, qk_heads, seq, 4)
    grow = col[:, :v_heads].reshape(batch, nc, CHUNK, qk_heads, 2).transpose(0, 3, 1, 4, 2)
    grow = grow.reshape(batch, qk_heads, nc, 2 * CHUNK)
    gated = gdn_mixer(proj, conv_w, head_norm, gcol, grow, batch=batch, seq=seq, qk_heads=qk_heads)
    return out_proj_residual(gated, w_out.astype(BF16), post, h, tm=512, name="gdn_out_proj")


def kernel(x, positions, hgrn_lb, l0_pre_norm, l0_post_norm, l0_w_in, l0_head_norm, l0_w_out, l1_pre_norm, l1_post_norm, l1_w_in, l1_q_norm, l1_kv_norm, l1_w_uq, l1_w_ukv, l1_w_out, l2_pre_norm, l2_post_norm, l2_w_in, l2_conv_w, l2_a_log, l2_dt_bias, l2_head_norm, l2_w_out, l3_pre_norm, l3_post_norm, l3_w_in, l3_head_norm, l3_w_out):
    batch, seq, d_model = x.shape
    h = x.reshape(batch * seq, d_model)
    h = _hgrn_layer(h, hgrn_lb, l0_pre_norm, l0_post_norm, l0_w_in, l0_head_norm, l0_w_out,
                    batch=batch, seq=seq, layer=0)
    h = _mla_layer(h, positions, l1_pre_norm, l1_post_norm, l1_w_in, l1_q_norm, l1_kv_norm,
                   l1_w_uq, l1_w_ukv, l1_w_out, batch=batch, seq=seq)
    h = _gdn_layer(h, l2_pre_norm, l2_post_norm, l2_w_in, l2_conv_w, l2_a_log, l2_dt_bias,
                   l2_head_norm, l2_w_out, batch=batch, seq=seq)
    h = _hgrn_layer(h, hgrn_lb, l3_pre_norm, l3_post_norm, l3_w_in, l3_head_norm, l3_w_out,
                    batch=batch, seq=seq, layer=3)
    return h.reshape(batch, seq, d_model)
```

```python
import functools
import math

import jax
import jax.numpy as jnp
from jax import lax
from jax.experimental import pallas as pl
from jax.experimental.pallas import tpu as pltpu

F32 = jnp.float32
BF16 = jnp.bfloat16

EPS = 1e-6
CHUNK = 64
SUB = 16
N_SUB = CHUNK // SUB
HEAD_DIM = 128
DEPTH = 4
ROPE_HALF = 32
ROPE_THETA = 10000.0
MLA_SCALE = (128 + 64) ** -0.5
CONV_WIDTH = 4
CONV_PAD = 8
MAX_EXP_ARG = 80.0
VMEM_LIMIT_BYTES = 56 * 1024 * 1024
HGRN_UNROLL = 16
GDN_PREP_UNROLL = 4
GDN_INVERT_UNROLL = 16
GDN_COMBINE_UNROLL = 8
GDN_RECUR_UNROLL = 8
MLA_TQ = 256
MLA_TK = 512
MLA_HEADS_PER_STEP = 4

NT_DIMS = (((1,), (1,)), ((), ()))
TN_DIMS = (((0,), (0,)), ((), ()))


def _params(*semantics):
    return pltpu.CompilerParams(dimension_semantics=semantics,
                                vmem_limit_bytes=VMEM_LIMIT_BYTES)


def _sigmoid(x):
    return 1.0 / (1.0 + jnp.exp(-x))


def _silu(x):
    return x * _sigmoid(x)


def _rms(x, gain):
    ms = jnp.mean(x * x, axis=-1, keepdims=True)
    return x * lax.rsqrt(ms + EPS) * gain


def _dot(a, b):
    return jnp.dot(a, b, preferred_element_type=F32)


def _dot_nt(a, b):
    return lax.dot_general(a, b, NT_DIMS, preferred_element_type=F32)


def _dot_tn(a, b):
    return lax.dot_general(a, b, TN_DIMS, preferred_element_type=F32)


def _stack(xs):
    m, n = xs[0].shape
    return jnp.concatenate(xs, axis=0).reshape(len(xs), m, n)


def _bdot(a, b):
    return lax.dot_general(a, b, (((2,), (1,)), ((0,), (0,))), preferred_element_type=F32)


def _bdot_nt(a, b):
    return lax.dot_general(a, b, (((2,), (2,)), ((0,), (0,))), preferred_element_type=F32)


def _bdot_tn(a, b):
    return lax.dot_general(a, b, (((1,), (1,)), ((0,), (0,))), preferred_element_type=F32)


def _chunk_cumsum(x, row):
    shift = 1
    while shift < CHUNK:
        x = x + jnp.where(row >= shift, pltpu.roll(x, shift, axis=0), 0.0)
        shift *= 2
    return x


def _norm_matmul_body(x_ref, g_ref, w_ref, o_ref, xn_ref):
    @pl.when(pl.program_id(1) == 0)
    def _():
        xn_ref[...] = _rms(x_ref[...], g_ref[...]).astype(BF16)

    o_ref[...] = _dot(xn_ref[...], w_ref[...].astype(BF16)).astype(o_ref.dtype)


def norm_matmul(x, col_block, gain, w, *, tm, tn, out_dtype, name, n_out=None):
    t = x.shape[0]
    k = w.shape[0]
    n = w.shape[1] if n_out is None else n_out
    tm, tn = min(tm, t), min(tn, n)
    assert t % tm == 0 and n % tn == 0 and n <= w.shape[1]
    return pl.pallas_call(
        _norm_matmul_body,
        grid=(t // tm, n // tn),
        in_specs=[pl.BlockSpec((tm, k), lambda i, j: (i, col_block)),
                  pl.BlockSpec((1, k), lambda i, j: (0, 0)),
                  pl.BlockSpec((k, tn), lambda i, j: (0, j))],
        out_specs=pl.BlockSpec((tm, tn), lambda i, j: (i, j)),
        out_shape=jax.ShapeDtypeStruct((t, n), out_dtype),
        scratch_shapes=[pltpu.VMEM((tm, k), BF16)],
        compiler_params=_params("parallel", "arbitrary"),
        name=name,
    )(x, gain.reshape(1, k), w)


def _out_proj_body(g_ref, w_ref, p_ref, h_ref, o_ref):
    y = _dot(g_ref[...], w_ref[...])
    o_ref[...] = h_ref[...] + _rms(y, p_ref[...])


def out_proj_residual(g, w, post, h, *, tm, name):
    t, k = g.shape
    d = w.shape[1]
    tm = min(tm, t)
    return pl.pallas_call(
        _out_proj_body,
        grid=(t // tm,),
        in_specs=[pl.BlockSpec((tm, k), lambda i: (i, 0)),
                  pl.BlockSpec((k, d), lambda i: (0, 0), pipeline_mode=pl.Buffered(1)),
                  pl.BlockSpec((1, d), lambda i: (0, 0)),
                  pl.BlockSpec((tm, d), lambda i: (i, 0))],
        out_specs=pl.BlockSpec((tm, d), lambda i: (i, 0)),
        out_shape=jax.ShapeDtypeStruct((t, d), F32),
        compiler_params=_params("parallel"),
        name=name,
    )(g, w, post.reshape(1, d), h)


def _hgrn_body(lbp_ref, hn_ref, q_ref, f_ref, i_ref, z_ref, o_ref, *, layer, n_chunks):
    p = lbp_ref[...]
    e = jnp.exp(p - jnp.max(p, axis=0, keepdims=True))
    sm = e / jnp.sum(e, axis=0, keepdims=True)
    lb = jnp.zeros((1, HEAD_DIM), F32)
    for j in range(1, layer + 1):
        lb = lb + sm[j:j + 1]
    log_lb = jnp.log(lb)
    log_1m_lb = jnp.log(1.0 - lb)
    one_m_lb = 1.0 - lb
    hn = hn_ref[...]

    n_par = math.gcd(n_chunks, HGRN_UNROLL)
    span = n_par * CHUNK
    row = lax.broadcasted_iota(jnp.int32, (span, HEAD_DIM), 0) % CHUNK
    t_idx = lax.broadcasted_iota(jnp.int32, (CHUNK, N_SUB * CHUNK), 0)
    c_idx = lax.broadcasted_iota(jnp.int32, (CHUNK, N_SUB * CHUNK), 1)
    valid = ((c_idx // CHUNK) == (t_idx // SUB)) & ((c_idx % CHUNK) <= t_idx)
    scale = HEAD_DIM ** -0.5

    def step(i, state_t):
        rows = pl.ds(pl.multiple_of(i * span, span), span)
        q = q_ref[rows, :]
        f = f_ref[rows, :]
        v16 = i_ref[rows, :].astype(BF16)

        ef = jnp.exp(-jnp.abs(f))
        inv = 1.0 / (1.0 + ef)
        log_sig = jnp.minimum(f, 0.0) - jnp.log(1.0 + ef)
        cc = log_1m_lb + log_sig
        log_f = jnp.maximum(log_lb, cc) + jnp.log(1.0 + jnp.exp(-jnp.abs(log_lb - cc)))
        k = one_m_lb * jnp.where(f >= 0.0, ef * inv, inv)
        qs = _silu(q) * scale
        b = _chunk_cumsum(log_f, row)

        q_in, k_stack, v_stack, q_out, k_out, decay_last = [], [], [], [], [], []
        for j in range(n_par):
            lo = j * CHUNK
            bj = b[lo:lo + CHUNK]
            kj = k[lo:lo + CHUNK]
            qj = qs[lo:lo + CHUNK]
            refs = [jnp.zeros((1, HEAD_DIM), F32)] + [bj[SUB * s - 1:SUB * s] for s in range(1, N_SUB)]
            ref_rows = jnp.concatenate([jnp.broadcast_to(r, (SUB, HEAD_DIM)) for r in refs], axis=0)
            q_in.append((qj * jnp.exp(bj - ref_rows)).astype(BF16))
            k_stack.append(jnp.concatenate(
                [kj * jnp.exp(jnp.minimum(r - bj, MAX_EXP_ARG)) for r in refs], axis=0).astype(BF16))
            v_stack.append(jnp.concatenate([v16[lo:lo + CHUNK]] * N_SUB, axis=0))
            q_out.append((qj * jnp.exp(bj)).astype(BF16))
            b_last = bj[CHUNK - 1:CHUNK]
            k_out.append((kj * jnp.exp(b_last - bj)).astype(BF16))
            decay_last.append(jnp.exp(b_last))

        scores = _bdot_nt(_stack(q_in), _stack(k_stack))
        scores = jnp.where(valid, scores, 0.0).astype(BF16)
        o = _bdot(scores, _stack(v_stack))
        d_state = _bdot_tn(v16.reshape(n_par, CHUNK, HEAD_DIM), _stack(k_out))
        states = [state_t]
        for j in range(n_par):
            states.append(states[j] * decay_last[j] + d_state[j])
        o = o + _bdot_nt(_stack(q_out), _stack([s.astype(BF16) for s in states[:n_par]]))

        o = _rms(o.reshape(span, HEAD_DIM), hn) * _silu(z_ref[rows, :])
        o_ref[rows, :] = o.astype(o_ref.dtype)
        return states[n_par]

    lax.fori_loop(0, n_chunks // n_par, step, jnp.zeros((HEAD_DIM, HEAD_DIM), F32))


def hgrn_mixer(proj, lb_param, head_norm, *, batch, seq, layer):
    heads = lb_param.shape[1] // HEAD_DIM
    col = lambda off: pl.BlockSpec((seq, HEAD_DIM), lambda b, h: (b, off * heads + h))
    return pl.pallas_call(
        functools.partial(_hgrn_body, layer=layer, n_chunks=seq // CHUNK),
        grid=(batch, heads),
        in_specs=[pl.BlockSpec((DEPTH, HEAD_DIM), lambda b, h: (0, h)),
                  pl.BlockSpec((1, HEAD_DIM), lambda b, h: (0, 0)),
                  col(0), col(1), col(2), col(3)],
        out_specs=pl.BlockSpec((seq, HEAD_DIM), lambda b, h: (b, h)),
        out_shape=jax.ShapeDtypeStruct((batch * seq, heads * HEAD_DIM), BF16),
        compiler_params=_params("parallel", "parallel"),
        name="hgrn_gla_l%d" % layer,
    )(lb_param, head_norm.reshape(1, HEAD_DIM), proj, proj, proj, proj)


def _rope_tables(pos_col):
    lane = lax.broadcasted_iota(jnp.int32, (1, HEAD_DIM), 1)
    j = (lane % (2 * ROPE_HALF)).astype(F32)
    inv_freq = jnp.exp(j * (-2.0 * math.log(ROPE_THETA) / (2 * ROPE_HALF)))
    live = (lane % (2 * ROPE_HALF)) < ROPE_HALF
    ang = pos_col * inv_freq
    cos = jnp.where(live, jnp.cos(ang), 0.0)
    sin = jnp.where(live, jnp.sin(ang), 0.0)
    sin = jnp.where(lane < 2 * ROPE_HALF, -sin, sin)
    return cos, sin


def _apply_rope(t, cos, sin):
    return t * cos + pltpu.roll(t, 2 * ROPE_HALF, axis=1) * sin


def _rope_prep_body(pos_ref, kr_ref, cos_ref, sin_ref, k_ref):
    cos, sin = _rope_tables(pos_ref[...].astype(F32))
    cos_ref[...] = cos
    sin_ref[...] = sin
    k_ref[...] = _apply_rope(kr_ref[...], cos, sin).astype(BF16)


def rope_prep(pos_col, proj, kr_block, *, tm):
    t = pos_col.shape[0]
    tm = min(tm, t)
    tab = pl.BlockSpec((tm, HEAD_DIM), lambda i: (i, 0))
    return pl.pallas_call(
        _rope_prep_body,
        grid=(t // tm,),
        in_specs=[pl.BlockSpec((tm, 1), lambda i: (i, 0)),
                  pl.BlockSpec((tm, HEAD_DIM), lambda i: (i, kr_block))],
        out_specs=[tab, tab, tab],
        out_shape=[jax.ShapeDtypeStruct((t, HEAD_DIM), F32),
                   jax.ShapeDtypeStruct((t, HEAD_DIM), F32),
                   jax.ShapeDtypeStruct((t, HEAD_DIM), BF16)],
        compiler_params=_params("parallel"),
        name="mla_rope_prep",
    )(pos_col, proj)


def _mla_attn_body(q_ref, cos_ref, sin_ref, kv_ref, kr_ref, z_ref, o_ref, *, tq, tk, n_heads):
    qi = pl.program_id(2)
    cos = cos_ref[...]
    sin = sin_ref[...]
    q_all = q_ref[...]
    q_cat = []
    for a in range(n_heads):
        q = q_all[:, 2 * a * HEAD_DIM:2 * (a + 1) * HEAD_DIM]
        q_rope = _apply_rope(q[:, HEAD_DIM:], cos, sin)
        q_cat.append((jnp.concatenate([q[:, :HEAD_DIM], q_rope], axis=1) * MLA_SCALE).astype(BF16))
    q_cat = _stack(q_cat)
    ones = jnp.ones((tk, HEAD_DIM), BF16)
    q_chunk = (qi * tq + lax.broadcasted_iota(jnp.int32, (tq, tk), 0)) // CHUNK
    k_in_block = lax.broadcasted_iota(jnp.int32, (tq, tk), 1)

    def block(j, carry, masked):
        m, acc = carry
        rows = pl.ds(pl.multiple_of(j * tk, tk), tk)
        kr = kr_ref[rows, :]
        k_cat, v_ext = [], []
        for a in range(n_heads):
            lo = 2 * a * HEAD_DIM
            k_cat.append(jnp.concatenate([kv_ref[rows, lo:lo + HEAD_DIM], kr], axis=1))
            v_ext.append(jnp.concatenate([kv_ref[rows, lo + HEAD_DIM:lo + 2 * HEAD_DIM], ones], axis=1))
        s = _bdot_nt(q_cat, _stack(k_cat))
        if masked:
            s = jnp.where((j * tk + k_in_block) // CHUNK <= q_chunk, s, -jnp.inf)
        m_new = jnp.maximum(m, jnp.max(s, axis=-1, keepdims=True))
        p = jnp.exp(s - m_new).astype(BF16)
        acc = jnp.exp(m - m_new) * acc + _bdot(p, _stack(v_ext))
        return m_new, acc

    init = (jnp.full((n_heads, tq, 1), -jnp.inf, F32), jnp.zeros((n_heads, tq, 2 * HEAD_DIM), F32))
    n_full = (qi * tq) // tk
    carry = lax.fori_loop(0, n_full, lambda j, c: block(j, c, False), init)
    _, acc = block(n_full, carry, True)
    z = z_ref[...]
    outs = [acc[a, :, :HEAD_DIM] / acc[a, :, HEAD_DIM:] * _silu(z[:, a * HEAD_DIM:(a + 1) * HEAD_DIM])
            for a in range(n_heads)]
    o_ref[...] = jnp.concatenate(outs, axis=1).astype(o_ref.dtype)


def mla_attention(q_up, cos, sin, kv_up, k_rope, proj, z_col0, *, batch, seq, heads):
    tq, tk, hps = min(MLA_TQ, seq), min(MLA_TK, seq), MLA_HEADS_PER_STEP
    assert seq % tk == 0 and tk % tq == 0 and tq % CHUNK == 0 and heads % hps == 0
    assert z_col0 % (hps * HEAD_DIM) == 0
    nq = seq // tq
    z_block0 = z_col0 // (hps * HEAD_DIM)
    return pl.pallas_call(
        functools.partial(_mla_attn_body, tq=tq, tk=tk, n_heads=hps),
        grid=(batch, heads // hps, nq),
        in_specs=[pl.BlockSpec((tq, hps * 2 * HEAD_DIM), lambda b, h, i: (b * nq + i, h)),
                  pl.BlockSpec((tq, HEAD_DIM), lambda b, h, i: (b * nq + i, 0)),
                  pl.BlockSpec((tq, HEAD_DIM), lambda b, h, i: (b * nq + i, 0)),
                  pl.BlockSpec((seq, hps * 2 * HEAD_DIM), lambda b, h, i: (b, h)),
                  pl.BlockSpec((seq, HEAD_DIM), lambda b, h, i: (b, 0)),
                  pl.BlockSpec((tq, hps * HEAD_DIM), lambda b, h, i: (b * nq + i, z_block0 + h))],
        out_specs=pl.BlockSpec((tq, hps * HEAD_DIM), lambda b, h, i: (b * nq + i, h)),
        out_shape=jax.ShapeDtypeStruct((batch * seq, heads * HEAD_DIM), BF16),
        compiler_params=_params("parallel", "parallel", "arbitrary"),
        name="mla_attention",
    )(q_up, cos, sin, kv_up, k_rope, proj)


def _gdn_gates_body(ab_ref, alog_ref, dt_ref, o_ref, *, n_heads):
    ab = ab_ref[...]
    x = ab[:, :n_heads] + dt_ref[...]
    softplus = jnp.maximum(x, 0.0) + jnp.log(1.0 + jnp.exp(-jnp.abs(x)))
    g = -jnp.exp(alog_ref[...]) * softplus
    row = lax.broadcasted_iota(jnp.int32, g.shape, 0) % CHUNK
    o_ref[...] = jnp.concatenate([_chunk_cumsum(g, row), _sigmoid(ab[:, n_heads:2 * n_heads])], axis=1)


def gdn_gates(ab, a_log, dt_bias, *, batch, seq):
    n_heads = a_log.shape[0]
    vec = pl.BlockSpec((1, n_heads), lambda b: (0, 0))
    return pl.pallas_call(
        functools.partial(_gdn_gates_body, n_heads=n_heads),
        grid=(batch,),
        in_specs=[pl.BlockSpec((seq, ab.shape[1]), lambda b: (b, 0)), vec, vec],
        out_specs=pl.BlockSpec((seq, 2 * n_heads), lambda b: (b, 0)),
        out_shape=jax.ShapeDtypeStruct((batch * seq, 2 * n_heads), F32),
        compiler_params=_params("parallel"),
        name="gdn_gates",
    )(ab, a_log.reshape(1, n_heads), dt_bias.reshape(1, n_heads))


def _block_diag2(x, left):
    return jnp.concatenate([jnp.where(left, x, 0.0), jnp.where(left, 0.0, x)], axis=1)


def _tri_inverse_pair(low, eye, left):
    m = -low
    s = eye + m
    p = _bdot(m.astype(BF16), _block_diag2(m, left).astype(BF16))
    n = 2
    while 2 * n < CHUNK:
        rhs = jnp.concatenate([_block_diag2(p, left), _block_diag2(s, left)], axis=2).astype(BF16)
        out = _bdot(p.astype(BF16), rhs)
        p = out[:, :, :2 * CHUNK]
        s = s + out[:, :, 2 * CHUNK:]
        n *= 2
    return s + _bdot(p.astype(BF16), _block_diag2(s, left).astype(BF16))


def _gdn_body(cw_ref, hn_ref, q_ref, k_ref, v_ref, z_ref, gcol_ref, grow_ref, o_ref,
              at_s, low_s, rhs_s, qe_s, ko_s, inv_s, lhs_s, n_s, op_s, *, n_chunks):
    cw = cw_ref[...]
    hn = hn_ref[...]
    row = lax.broadcasted_iota(jnp.int32, (CHUNK, 2 * CHUNK), 0)
    lane = lax.broadcasted_iota(jnp.int32, (CHUNK, 2 * CHUNK), 1)
    left = lane < CHUNK
    col = lane % CHUNK
    tri = col <= row
    strict = col < row
    eye = (col == row).astype(F32)
    zeros_wu = jnp.zeros((CHUNK, 2 * HEAD_DIM), BF16)

    def conv_silu(x_ref, c, w):
        start = pl.multiple_of(c * CHUNK, CHUNK)
        cur = x_ref[pl.ds(start, CHUNK), :]
        prev_start = pl.multiple_of(jnp.maximum(start - CONV_PAD, 0), CONV_PAD)
        prev = x_ref[pl.ds(prev_start, CONV_PAD), :] * jnp.where(c > 0, 1.0, 0.0)
        ext = jnp.concatenate([prev, cur], axis=0)
        y = cur * w[CONV_WIDTH - 1:CONV_WIDTH]
        for d in range(1, CONV_WIDTH):
            y = y + ext[CONV_PAD - d:CONV_PAD - d + CHUNK] * w[CONV_WIDTH - 1 - d:CONV_WIDTH - d]
        return _silu(y)

    def l2n(x):
        return x * lax.rsqrt(jnp.sum(x * x, axis=-1, keepdims=True) + EPS)

    n_prep = math.gcd(n_chunks, GDN_PREP_UNROLL)
    n_inv = math.gcd(n_chunks, GDN_INVERT_UNROLL)

    def prepare(i, _):
        lhs, rhs, kept = [], [], []
        for j in range(n_prep):
            c = i * n_prep + j
            q = l2n(conv_silu(q_ref, c, cw[:, :HEAD_DIM])) * (HEAD_DIM ** -0.5)
            k = l2n(conv_silu(k_ref, c, cw[:, HEAD_DIM:2 * HEAD_DIM]))
            v = conv_silu(v_ref, c, cw[:, 2 * HEAD_DIM:])
            q16 = q.astype(BF16)
            k16 = k.astype(BF16)
            lhs.append(jnp.concatenate([q16, k16], axis=0))
            rhs.append(jnp.concatenate([k16, k16], axis=0))
            kept.append((c, q, k, v))
        qk_kk = _bdot_nt(_stack(lhs), _stack(rhs))
        for j, (c, q, k, v) in enumerate(kept):
            rows = pl.ds(pl.multiple_of(c * CHUNK, CHUNK), CHUNK)
            gcol = gcol_ref[0, 0, rows, :]
            wide = [jnp.broadcast_to(gcol[:, n:n + 1], (CHUNK, HEAD_DIM)) for n in range(4)]
            g_pair = jnp.where(left, wide[0], wide[1])
            beta_pair = jnp.where(left, wide[2], wide[3])
            g_row = grow_ref[0, 0, pl.ds(c, 1), :]
            decay = jnp.exp(jnp.minimum(g_pair - g_row, 0.0))
            at_s[rows, :] = jnp.where(tri, qk_kk[j, :CHUNK] * decay, 0.0).astype(BF16)
            low_s[rows, :] = jnp.where(strict, qk_kk[j, CHUNK:] * decay, 0.0) * beta_pair
            for hh in range(2):
                gc = wide[hh]
                beta = wide[2 + hh]
                e_g = jnp.exp(gc)
                rhs_s[hh, rows, :] = jnp.concatenate([v[:, hh * HEAD_DIM:(hh + 1) * HEAD_DIM] * beta,
                                                      k * (beta * e_g)], axis=1).astype(BF16)
                qe_s[hh, rows, :] = q * e_g
                ko_s[hh, rows, :] = (k * jnp.exp(gc[CHUNK - 1:CHUNK] - gc)).astype(BF16)
        return 0

    def invert(i, _):
        rows = pl.ds(pl.multiple_of(i * (n_inv * CHUNK), n_inv * CHUNK), n_inv * CHUNK)
        low = low_s[rows, :].reshape(n_inv, CHUNK, 2 * CHUNK)
        inv = _tri_inverse_pair(low, eye, left)
        inv_s[rows, :] = inv.reshape(n_inv * CHUNK, 2 * CHUNK).astype(BF16)
        return 0

    def block_diag_wide(x0, x1):
        return jnp.concatenate([jnp.concatenate([x0, zeros_wu], axis=1),
                                jnp.concatenate([zeros_wu, x1], axis=1)], axis=0)

    n_comb = math.gcd(n_chunks, GDN_COMBINE_UNROLL)

    def combine(i, _):
        span = n_comb * CHUNK
        rows_all = pl.ds(pl.multiple_of(i * span, span), span)
        chunk_rows = [pl.ds(pl.multiple_of((i * n_comb + j) * CHUNK, CHUNK), CHUNK) for j in range(n_comb)]
        rhs_bd = _stack([block_diag_wide(rhs_s[0, r, :], rhs_s[1, r, :]) for r in chunk_rows])
        inv = inv_s[rows_all, :].reshape(n_comb, CHUNK, 2 * CHUNK)
        uw = _bdot(inv, rhs_bd).astype(BF16)
        uw_bd = _stack([block_diag_wide(uw[j, :, :2 * HEAD_DIM], uw[j, :, 2 * HEAD_DIM:])
                        for j in range(n_comb)])
        attn = at_s[rows_all, :].reshape(n_comb, CHUNK, 2 * CHUNK)
        a_uw = _bdot(attn, uw_bd)
        k_out = _stack([ko_s[hh, r, :] for r in chunk_rows for hh in range(2)])
        uw_h = _stack([uw[j, :, 2 * hh * HEAD_DIM:2 * (hh + 1) * HEAD_DIM]
                       for j in range(n_comb) for hh in range(2)])
        kt_uw = _bdot_tn(k_out, uw_h)
        for j in range(n_comb):
            c = i * n_comb + j
            base = pl.multiple_of(c * (HEAD_DIM + CHUNK), HEAD_DIM + CHUNK)
            n_rows = pl.ds(pl.multiple_of(c * HEAD_DIM, HEAD_DIM), HEAD_DIM)
            for hh in range(2):
                lo = 2 * hh * HEAD_DIM
                kt = kt_uw[2 * j + hh]
                q_eff = qe_s[hh, chunk_rows[j], :] - a_uw[j, :, lo + HEAD_DIM:lo + 2 * HEAD_DIM]
                lhs_s[hh, pl.ds(base, HEAD_DIM), :] = kt[:, HEAD_DIM:].astype(BF16)
                lhs_s[hh, pl.ds(base + HEAD_DIM, CHUNK), :] = q_eff.astype(BF16)
                n_s[hh, n_rows, :] = kt[:, :HEAD_DIM]
                op_s[hh, chunk_rows[j], :] = a_uw[j, :, lo:lo + HEAD_DIM]
        return 0

    lax.fori_loop(0, n_chunks // n_prep, prepare, 0)
    lax.fori_loop(0, n_chunks // n_inv, invert, 0)
    lax.fori_loop(0, n_chunks // n_comb, combine, 0)

    def recur(c, states):
        rows = pl.ds(pl.multiple_of(c * CHUNK, CHUNK), CHUNK)
        z = z_ref[rows, :]
        base = pl.multiple_of(c * (HEAD_DIM + CHUNK), HEAD_DIM + CHUNK)
        n_rows = pl.ds(pl.multiple_of(c * HEAD_DIM, HEAD_DIM), HEAD_DIM)
        pr = _bdot(lhs_s[:, pl.ds(base, HEAD_DIM + CHUNK), :],
                   _stack([st.astype(BF16) for st in states]))
        outs = []
        new_states = []
        for hh in range(2):
            o = pr[hh, HEAD_DIM:] + op_s[hh, rows, :]
            g_last = gcol_ref[0, 0, pl.ds(c * CHUNK + CHUNK - 1, 1), hh:hh + 1]
            new_states.append(states[hh] * jnp.exp(g_last) + (n_s[hh, n_rows, :] - pr[hh, :HEAD_DIM]))
            outs.append(_rms(o, hn) * _silu(z[:, hh * HEAD_DIM:(hh + 1) * HEAD_DIM]))
        o_ref[rows, :] = jnp.concatenate(outs, axis=1).astype(o_ref.dtype)
        return tuple(new_states)

    zero = jnp.zeros((HEAD_DIM, HEAD_DIM), F32)
    lax.fori_loop(0, n_chunks, recur, (zero, zero), unroll=math.gcd(n_chunks, GDN_RECUR_UNROLL))


def gdn_mixer(proj, conv_w, head_norm, gcol, grow, *, batch, seq, qk_heads):
    nc = seq // CHUNK
    key_w = qk_heads * HEAD_DIM
    cw = jnp.concatenate([
        conv_w[:, :key_w].reshape(CONV_WIDTH, qk_heads, HEAD_DIM),
        conv_w[:, key_w:2 * key_w].reshape(CONV_WIDTH, qk_heads, HEAD_DIM),
        conv_w[:, 2 * key_w:].reshape(CONV_WIDTH, qk_heads, 2 * HEAD_DIM)], axis=2)
    cw = cw.transpose(1, 0, 2)
    v_block0 = 2 * qk_heads // 2
    z_block0 = v_block0 + qk_heads
    assert seq % CHUNK == 0 and gcol.shape == (batch, qk_heads, seq, 4)
    assert grow.shape == (batch, qk_heads, nc, 2 * CHUNK)
    return pl.pallas_call(
        functools.partial(_gdn_body, n_chunks=nc),
        grid=(batch, qk_heads),
        in_specs=[pl.BlockSpec((None, CONV_WIDTH, 4 * HEAD_DIM), lambda b, g: (g, 0, 0)),
                  pl.BlockSpec((1, HEAD_DIM), lambda b, g: (0, 0)),
                  pl.BlockSpec((seq, HEAD_DIM), lambda b, g: (b, g)),
                  pl.BlockSpec((seq, HEAD_DIM), lambda b, g: (b, qk_heads + g)),
                  pl.BlockSpec((seq, 2 * HEAD_DIM), lambda b, g: (b, v_block0 + g)),
                  pl.BlockSpec((seq, 2 * HEAD_DIM), lambda b, g: (b, z_block0 + g)),
                  pl.BlockSpec((1, 1, seq, 4), lambda b, g: (b, g, 0, 0)),
                  pl.BlockSpec((1, 1, nc, 2 * CHUNK), lambda b, g: (b, g, 0, 0))],
        out_specs=pl.BlockSpec((seq, 2 * HEAD_DIM), lambda b, g: (b, g)),
        out_shape=jax.ShapeDtypeStruct((batch * seq, 2 * qk_heads * HEAD_DIM), BF16),
        scratch_shapes=[pltpu.VMEM((seq, 2 * CHUNK), BF16),
                        pltpu.VMEM((seq, 2 * CHUNK), F32),
                        pltpu.VMEM((2, seq, 2 * HEAD_DIM), BF16),
                        pltpu.VMEM((2, seq, HEAD_DIM), F32),
                        pltpu.VMEM((2, seq, HEAD_DIM), BF16),
                        pltpu.VMEM((seq, 2 * CHUNK), BF16),
                        pltpu.VMEM((2, nc * (HEAD_DIM + CHUNK), HEAD_DIM), BF16),
                        pltpu.VMEM((2, nc * HEAD_DIM, HEAD_DIM), F32),
                        pltpu.VMEM((2, seq, HEAD_DIM), F32)],
        compiler_params=_params("parallel", "parallel"),
        name="gdn_delta_rule",
    )(cw, head_norm.reshape(1, HEAD_DIM), proj, proj, proj, proj, gcol, grow)


def _hgrn_layer(h, lb_param, pre, post, w_in, head_norm, w_out, *, batch, seq, layer):
    proj = norm_matmul(h, 0, pre, w_in, tm=1024, tn=1024, out_dtype=F32,
                       name="hgrn_in_proj_l%d" % layer)
    gated = hgrn_mixer(proj, lb_param, head_norm, batch=batch, seq=seq, layer=layer)
    return out_proj_residual(gated, w_out.astype(BF16), post, h, tm=512,
                             name="hgrn_out_proj_l%d" % layer)


def _pad_rope_cols(w):
    zeros = jnp.zeros(w.shape[:-1] + (ROPE_HALF,), w.dtype)
    return jnp.concatenate([w[..., :ROPE_HALF], zeros, w[..., ROPE_HALF:], zeros], axis=-1)


def _mla_layer(h, positions, pre, post, w_in, q_norm, kv_norm, w_uq, w_ukv, w_out, *, batch, seq):
    d_model = h.shape[1]
    q_rank = q_norm.shape[0]
    kv_rank = kv_norm.shape[0]
    heads = w_out.shape[0] // HEAD_DIM
    lat = q_rank + kv_rank
    gate_w = heads * HEAD_DIM
    w_in_p = jnp.concatenate([w_in[:, :lat], w_in[:, lat + 2 * ROPE_HALF:],
                              _pad_rope_cols(w_in[:, lat:lat + 2 * ROPE_HALF])], axis=1)
    proj = norm_matmul(h, 0, pre, w_in_p, tm=1024, tn=640, out_dtype=F32, name="mla_in_proj")
    w_uq_h = w_uq.reshape(q_rank, heads, HEAD_DIM + 2 * ROPE_HALF)
    w_uq_p = jnp.concatenate([w_uq_h[..., :HEAD_DIM], _pad_rope_cols(w_uq_h[..., HEAD_DIM:])],
                             axis=-1).reshape(q_rank, heads * 2 * HEAD_DIM).astype(BF16)
    q_up = norm_matmul(proj, 0, q_norm, w_uq_p, tm=1024, tn=1024, out_dtype=F32, name="mla_q_up")
    kv_up = norm_matmul(proj, kv_rank // q_rank, kv_norm, w_ukv.astype(BF16), tm=1024, tn=1024,
                        out_dtype=BF16, name="mla_kv_up")
    pos_col = positions.reshape(batch * seq, 1)
    cos, sin, k_rope = rope_prep(pos_col, proj, (lat + gate_w) // HEAD_DIM, tm=1024)
    gated = mla_attention(q_up, cos, sin, kv_up, k_rope, proj, lat, batch=batch, seq=seq, heads=heads)
    del d_model
    return out_proj_residual(gated, w_out.astype(BF16), post, h, tm=512, name="mla_out_proj")


def _gdn_layer(h, pre, post, w_in, conv_w, a_log, dt_bias, head_norm, w_out, *, batch, seq):
    v_heads = a_log.shape[0]
    qk_heads = v_heads // 2
    main = 2 * qk_heads * HEAD_DIM + 2 * v_heads * HEAD_DIM
    proj = norm_matmul(h, 0, pre, w_in, n_out=main, tm=1024, tn=1024, out_dtype=F32,
                       name="gdn_in_proj")
    w_ab = jnp.pad(w_in[:, main:], ((0, 0), (0, HEAD_DIM - 2 * v_heads))).astype(BF16)
    ab = norm_matmul(h, 0, pre, w_ab, tm=1024, tn=HEAD_DIM, out_dtype=F32,
                     name="gdn_gate_proj")
    nc = seq // CHUNK
    col = gdn_gates(ab, a_log, dt_bias, batch=batch, seq=seq)
    gcol = col.reshape(batch, seq, 2, qk_heads, 2).transpose(0, 3, 1, 2, 4)
    gcol = gcol.reshape(batch, qk_heads, seq, 4)
    grow = col[:, :v_heads].reshape(batch, nc, CHUNK, qk_heads, 2).transpose(0, 3, 1, 4, 2)
    grow = grow.reshape(batch, qk_heads, nc, 2 * CHUNK)
    gated = gdn_mixer(proj, conv_w, head_norm, gcol, grow, batch=batch, seq=seq, qk_heads=qk_heads)
    return out_proj_residual(gated, w_out.astype(BF16), post, h, tm=512, name="gdn_out_proj")


def kernel(x, positions, hgrn_lb, l0_pre_norm, l0_post_norm, l0_w_in, l0_head_norm, l0_w_out, l1_pre_norm, l1_post_norm, l1_w_in, l1_q_norm, l1_kv_norm, l1_w_uq, l1_w_ukv, l1_w_out, l2_pre_norm, l2_post_norm, l2_w_in, l2_conv_w, l2_a_log, l2_dt_bias, l2_head_norm, l2_w_out, l3_pre_norm, l3_post_norm, l3_w_in, l3_head_norm, l3_w_out):
    batch, seq, d_model = x.shape
    h = x.reshape(batch * seq, d_model)
    h = _hgrn_layer(h, hgrn_lb, l0_pre_norm, l0_post_norm, l0_w_in, l0_head_norm, l0_w_out,
                    batch=batch, seq=seq, layer=0)
    h = _mla_layer(h, positions, l1_pre_norm, l1_post_norm, l1_w_in, l1_q_norm, l1_kv_norm,
                   l1_w_uq, l1_w_ukv, l1_w_out, batch=batch, seq=seq)
    h = _gdn_layer(h, l2_pre_norm, l2_post_norm, l2_w_in, l2_conv_w, l2_a_log, l2_dt_bias,
                   l2_head_norm, l2_w_out, batch=batch, seq=seq)
    h = _hgrn_layer(h, hgrn_lb, l3_pre_norm, l3_post_norm, l3_w_in, l3_head_norm, l3_w_out,
                    batch=batch, seq=seq, layer=3)
    return h.reshape(batch, seq, d_model)
```
